```python
import jax, jax.numpy as jnp
from jax import lax
import numpy as np

D_MODEL = 1024
BATCH = 8
SEQ = 4096
DEPTH = 2

HEAD_DIM = 64
ATTN_Q_HEADS = 8
ATTN_KV_HEADS = 2
WINDOW = 128
SGU_GROUPS = 4
SGU_GROUP_DIM = 64
SGU_CHUNK = 128
GLA_HEADS = 4
GLA_KEY_DIM = 64
GLA_VALUE_DIM = 64
GLA_GATE_RANK = 16
GLA_TAU = 16.0
GLA_CHUNK = 64
D_FF = 2816
RMS_EPS = 1e-6
NEG_INF = -1e30

ATTN_Q_W = ATTN_Q_HEADS * HEAD_DIM
ATTN_KV_W = ATTN_KV_HEADS * HEAD_DIM
SGU_W = SGU_GROUPS * SGU_GROUP_DIM
GLA_QK_W = GLA_HEADS * GLA_KEY_DIM
GLA_V_W = GLA_HEADS * GLA_VALUE_DIM
MIX_W = ATTN_Q_W + SGU_W + GLA_V_W
IN_SPLITS = (ATTN_Q_W, ATTN_KV_W, ATTN_KV_W, SGU_W, SGU_W,
             GLA_QK_W, GLA_QK_W, GLA_V_W, GLA_V_W, GLA_GATE_RANK)
IN_W = ATTN_Q_W + 2 * ATTN_KV_W + 2 * SGU_W + 2 * GLA_QK_W + 2 * GLA_V_W + GLA_GATE_RANK

kernel_name = "hymba_style_swa_sgu_gla_macaron"


def rms_norm(x, g):
    xf = x.astype(jnp.float32)
    y = xf * lax.rsqrt(jnp.mean(xf * xf, axis=-1, keepdims=True) + RMS_EPS)
    return (y * g.astype(jnp.float32)).astype(x.dtype)


def swiglu_ffn(h, w_gate, w_up, w_down):
    return (jax.nn.silu(h @ w_gate) * (h @ w_up)) @ w_down


def split_columns(p):
    outs, start = [], 0
    for size in IN_SPLITS:
        outs.append(p[..., start:start + size])
        start += size
    return outs


def sliding_window_sink_attention(q, k, v, sinks):
    B, T, H, Dh = q.shape
    Hkv = k.shape[2]
    G = H // Hkv
    nb = T // WINDOW
    qb = q.astype(jnp.float32).reshape(B, nb, WINDOW, Hkv, G, Dh)

    def banded(t):
        cur = t.astype(jnp.float32).reshape(B, nb, WINDOW, Hkv, Dh)
        prev = jnp.pad(cur, ((0, 0), (1, 0), (0, 0), (0, 0), (0, 0)))[:, :-1]
        return jnp.concatenate([prev, cur], axis=2)

    kb, vb = banded(k), banded(v)
    scores = jnp.einsum('bnqhgd,bnkhd->bnhgqk', qb, kb) * (Dh ** -0.5)
    qi = jnp.arange(WINDOW)[:, None]
    kj = jnp.arange(2 * WINDOW)[None, :]
    rel = qi + WINDOW - kj
    band = (rel >= 0) & (rel < WINDOW)
    blk = jnp.arange(nb)[:, None, None]
    mask = band[None] & ((blk > 0) | (kj[None] >= WINDOW))
    scores = jnp.where(mask[None, :, None, None], scores, NEG_INF)
    sink = jnp.broadcast_to(sinks.astype(jnp.float32).reshape(1, 1, Hkv, G, 1, 1),
                            scores.shape[:-1] + (1,))
    probs = jax.nn.softmax(jnp.concatenate([scores, sink], axis=-1), axis=-1)[..., :-1]
    out = jnp.einsum('bnhgqk,bnkhd->bnqhgd', probs, vb)
    return out.reshape(B, T, H * Dh).astype(v.dtype)


def chunked_spatial_gating(u, v, v_norm, w_s, b_s):
    B, T, _ = u.shape
    nc = T // SGU_CHUNK
    vg = v.reshape(B, T, SGU_GROUPS, SGU_GROUP_DIM)
    vn = rms_norm(vg, v_norm.reshape(SGU_GROUPS, SGU_GROUP_DIM))
    vc = vn.reshape(B, nc, SGU_CHUNK, SGU_GROUPS, SGU_GROUP_DIM).astype(jnp.float32)
    w_causal = jnp.tril(w_s.astype(jnp.float32))
    s = jnp.einsum('gts,bnsgc->bntgc', w_causal, vc) + b_s.astype(jnp.float32).T[None, None, :, :, None]
    return (u.astype(jnp.float32) * s.reshape(B, T, SGU_W)).astype(u.dtype)


def gated_linear_attention(q, k, v, g_log, gate, out_norm):
    B, T, H, dk = q.shape
    dv = v.shape[-1]
    N = T // GLA_CHUNK
    C = GLA_CHUNK

    def chunk(t):
        return t.astype(jnp.float32).reshape(B, N, C, H, t.shape[-1]).transpose(0, 3, 1, 2, 4)

    qc = chunk(q) * (dk ** -0.5)
    kc, vc, gc = chunk(k), chunk(v), chunk(g_log)
    b = jnp.cumsum(gc, axis=-2)
    b_last = b[..., -1:, :]
    q_dec = qc * jnp.exp(b)
    k_intra = kc * jnp.exp(-b)
    k_state = kc * jnp.exp(b_last - b)
    causal = jnp.tril(jnp.ones((C, C), dtype=bool))
    attn = jnp.where(causal, jnp.einsum('bhncd,bhnsd->bhncs', q_dec, k_intra), 0.0)
    o_intra = jnp.einsum('bhncs,bhnse->bhnce', attn, vc)
    delta = jnp.einsum('bhncd,bhnce->bhnde', k_state, vc)
    decay = jnp.exp(b_last[..., 0, :])

    def step(S, inp):
        dec, dlt = inp
        return dec[..., None] * S + dlt, S

    S0 = jnp.zeros((B, H, dk, dv), jnp.float32)
    _, S_prev = lax.scan(step, S0, (decay.transpose(2, 0, 1, 3), delta.transpose(2, 0, 1, 3, 4)))
    S_prev = S_prev.transpose(1, 2, 0, 3, 4)
    o_inter = jnp.einsum('bhncd,bhnde->bhnce', q_dec, S_prev)
    o = (o_intra + o_inter).transpose(0, 2, 3, 1, 4).reshape(B, T, H, dv)
    o = rms_norm(o, out_norm) * jax.nn.silu(gate.astype(jnp.float32))
    return o.reshape(B, T, H * dv).astype(v.dtype)


def setup_inputs(seed: int = 0) -> dict:
    key = jax.random.key(seed)
    ks = jax.random.split(key, 24)
    f32 = jnp.float32

    def nrm(k, shape, scale):
        return jax.random.normal(k, shape, f32) * scale

    def gain(k, shape):
        return 1.0 + 0.02 * jax.random.normal(k, shape, f32)

    L, D, F = DEPTH, D_MODEL, D_FF
    return {
        "x": nrm(ks[0], (BATCH, SEQ, D), 1.0),
        "ffn1_norm": gain(ks[1], (L, D)),
        "ffn1_w_gate": nrm(ks[2], (L, D, F), D ** -0.5),
        "ffn1_w_up": nrm(ks[3], (L, D, F), D ** -0.5),
        "ffn1_w_down": nrm(ks[4], (L, F, D), F ** -0.5),
        "mix_norm": gain(ks[5], (L, D)),
        "w_in": nrm(ks[6], (L, D, IN_W), D ** -0.5),
        "attn_q_norm": gain(ks[7], (L, HEAD_DIM)),
        "attn_k_norm": gain(ks[8], (L, HEAD_DIM)),
        "attn_sinks": nrm(ks[9], (L, ATTN_Q_HEADS), 1.0),
        "sgu_v_norm": gain(ks[10], (L, SGU_W)),
        "sgu_w": nrm(ks[11], (L, SGU_GROUPS, SGU_CHUNK, SGU_CHUNK), SGU_CHUNK ** -0.5),
        "sgu_b": 1.0 + 0.1 * jax.random.normal(ks[12], (L, SGU_GROUPS, SGU_CHUNK), f32),
        "gla_w_gate_up": nrm(ks[13], (L, GLA_GATE_RANK, GLA_QK_W), GLA_GATE_RANK ** -0.5),
        "gla_b_gate": nrm(ks[14], (L, GLA_QK_W), 0.1),
        "gla_out_norm": gain(ks[15], (L, GLA_VALUE_DIM)),
        "w_out": nrm(ks[16], (L, MIX_W, D), MIX_W ** -0.5),
        "ffn2_norm": gain(ks[17], (L, D)),
        "ffn2_w_gate": nrm(ks[18], (L, D, F), D ** -0.5),
        "ffn2_w_up": nrm(ks[19], (L, D, F), D ** -0.5),
        "ffn2_w_down": nrm(ks[20], (L, F, D), F ** -0.5),
    }


def reference(x, ffn1_norm, ffn1_w_gate, ffn1_w_up, ffn1_w_down, mix_norm, w_in,
              attn_q_norm, attn_k_norm, attn_sinks, sgu_v_norm, sgu_w, sgu_b,
              gla_w_gate_up, gla_b_gate, gla_out_norm, w_out,
              ffn2_norm, ffn2_w_gate, ffn2_w_up, ffn2_w_down):
    B, T, _ = x.shape
    for l in range(DEPTH):
        x = x + 0.5 * swiglu_ffn(rms_norm(x, ffn1_norm[l]), ffn1_w_gate[l], ffn1_w_up[l], ffn1_w_down[l])

        h = rms_norm(x, mix_norm[l])
        p = h @ w_in[l]
        a_q, a_k, a_v, s_u, s_v, c_q, c_k, c_v, c_g, c_lr = split_columns(p)

        a_q = rms_norm(a_q.reshape(B, T, ATTN_Q_HEADS, HEAD_DIM), attn_q_norm[l])
        a_k = rms_norm(a_k.reshape(B, T, ATTN_KV_HEADS, HEAD_DIM), attn_k_norm[l])
        a_v = a_v.reshape(B, T, ATTN_KV_HEADS, HEAD_DIM)
        out_a = sliding_window_sink_attention(a_q, a_k, a_v, attn_sinks[l])

        out_b = chunked_spatial_gating(jax.nn.gelu(s_u), jax.nn.gelu(s_v),
                                       sgu_v_norm[l], sgu_w[l], sgu_b[l])

        gate_logits = (c_lr @ gla_w_gate_up[l] + gla_b_gate[l]).astype(jnp.float32)
        g_log = jax.nn.log_sigmoid(gate_logits) / GLA_TAU
        out_c = gated_linear_attention(
            c_q.reshape(B, T, GLA_HEADS, GLA_KEY_DIM),
            c_k.reshape(B, T, GLA_HEADS, GLA_KEY_DIM),
            c_v.reshape(B, T, GLA_HEADS, GLA_VALUE_DIM),
            g_log.reshape(B, T, GLA_HEADS, GLA_KEY_DIM),
            c_g.reshape(B, T, GLA_HEADS, GLA_VALUE_DIM),
            gla_out_norm[l])

        mixed = jnp.concatenate([out_a, out_b.astype(out_a.dtype), out_c.astype(out_a.dtype)], axis=-1)
        x = x + (mixed @ w_out[l]).astype(x.dtype)

        x = x + 0.5 * swiglu_ffn(rms_norm(x, ffn2_norm[l]), ffn2_w_gate[l], ffn2_w_up[l], ffn2_w_down[l])
    return x
```

```python
import functools

import jax
import jax.numpy as jnp
from jax import lax
from jax.experimental import pallas as pl
from jax.experimental.pallas import tpu as pltpu

F32 = jnp.float32
BF16 = jnp.bfloat16

D_MODEL = 1024
D_FF = 2816
DEPTH = 2
HEAD_DIM = 64
ATTN_Q_HEADS = 8
ATTN_KV_HEADS = 2
ATTN_GROUP = ATTN_Q_HEADS // ATTN_KV_HEADS
WINDOW = 128
SGU_GROUPS = 4
SGU_CHUNK = 128
GLA_HEADS = 4
GLA_GATE_RANK = 16
GLA_TAU = 16.0
GLA_CHUNK = 64
RMS_EPS = 1e-6
NEG_INF = -1e30

ATTN_Q_W = ATTN_Q_HEADS * HEAD_DIM
ATTN_KV_W = ATTN_KV_HEADS * HEAD_DIM
SGU_W = SGU_GROUPS * HEAD_DIM
GLA_W = GLA_HEADS * HEAD_DIM
OFF_AQ = 0
OFF_AK = OFF_AQ + ATTN_Q_W
OFF_AV = OFF_AK + ATTN_KV_W
OFF_SU = OFF_AV + ATTN_KV_W
OFF_SV = OFF_SU + SGU_W
OFF_CQ = OFF_SV + SGU_W
OFF_CK = OFF_CQ + GLA_W
OFF_CV = OFF_CK + GLA_W
OFF_CG = OFF_CV + GLA_W
OFF_LR = OFF_CG + GLA_W
IN_MAIN_W = OFF_LR
MIX_W = ATTN_Q_W + SGU_W + GLA_W

FFN_TILE_M = 512
FFN_CHUNKS = 2
MIX_TILE_T = 512
VMEM_LIMIT_BYTES = 56 * 1024 * 1024


def _dot(a, b):
    return jnp.dot(a, b, preferred_element_type=F32)


def _dot_nt(a, b):
    return lax.dot_general(a, b, (((1,), (1,)), ((), ())), preferred_element_type=F32)


def _sigmoid(x):
    return 1.0 / (1.0 + jnp.exp(-x))


def _gelu_tanh(x):
    c = 0.7978845608028654
    return x * (0.5 * (1.0 + jnp.tanh(c * (x + 0.044715 * (x * x * x)))))


def _rms_rows(x, gain):
    ms = jnp.mean(x * x, axis=-1, keepdims=True)
    return x * lax.rsqrt(ms + RMS_EPS) * gain


def _group_ids(shape, dim, group):
    return lax.broadcasted_iota(jnp.int32, shape, dim) // group


def _group_rms(x, gain, same_group):
    ss = _dot((x * x).astype(BF16), same_group)
    return x * lax.rsqrt(ss * (1.0 / HEAD_DIM) + RMS_EPS) * gain


def _ffn_kernel(x_ref, gain_ref, wg_ref, wu_ref, wd_ref, o_ref):
    x = x_ref[...]
    h = _rms_rows(x, gain_ref[...]).astype(BF16)
    cw = D_FF // FFN_CHUNKS
    y = None
    for c in range(FFN_CHUNKS):
        sl = slice(c * cw, (c + 1) * cw)
        gate = _dot(h, wg_ref[:, sl])
        up = _dot(h, wu_ref[:, sl])
        act = (gate * _sigmoid(gate) * up).astype(BF16)
        part = _dot(act, wd_ref[sl, :])
        y = part if y is None else y + part
    o_ref[...] = x + 0.5 * y


def _resident(shape):
    nd = len(shape)
    return pl.BlockSpec(shape, lambda *_: (0,) * nd, pipeline_mode=pl.Buffered(1))


def _ffn(x2d, gain, wg, wu, wd):
    m = x2d.shape[0]
    return pl.pallas_call(
        _ffn_kernel,
        out_shape=jax.ShapeDtypeStruct((m, D_MODEL), F32),
        grid=(m // FFN_TILE_M,),
        in_specs=[
            pl.BlockSpec((FFN_TILE_M, D_MODEL), lambda i: (i, 0)),
            _resident((1, D_MODEL)),
            _resident((D_MODEL, D_FF)),
            _resident((D_MODEL, D_FF)),
            _resident((D_FF, D_MODEL)),
        ],
        out_specs=pl.BlockSpec((FFN_TILE_M, D_MODEL), lambda i: (i, 0)),
        compiler_params=pltpu.CompilerParams(
            dimension_semantics=("parallel",), vmem_limit_bytes=VMEM_LIMIT_BYTES),
        name="swiglu_ffn",
    )(x2d, gain, wg, wu, wd)


def _mixer_kernel(x_ref, gain_ref, w_in_ref, w_lr_ref, qg_ref, kg_ref, sink_ref, svg_ref,
                  sgu_w_ref, sgu_b_ref, gup_ref, gb_ref, og_ref, w_out_ref,
                  o_ref, kbuf, vbuf, state, mixed):
    tt = MIX_TILE_T
    ti = pl.program_id(1)

    @pl.when(ti > 0)
    def _():
        kbuf[0:WINDOW, :] = kbuf[tt:tt + WINDOW, :]
        vbuf[0:WINDOW, :] = vbuf[tt:tt + WINDOW, :]

    @pl.when(ti == 0)
    def _():
        kbuf[0:WINDOW, :] = jnp.zeros((WINDOW, 2 * GLA_W), BF16)
        vbuf[0:WINDOW, :] = jnp.zeros((WINDOW, 2 * GLA_W), BF16)
        state[...] = jnp.zeros_like(state)

    x = x_ref[...]
    h = _rms_rows(x, gain_ref[...]).astype(BF16)
    p = _dot(h, w_in_ref[...])
    lr = _dot(h, w_lr_ref[...])

    same256 = (_group_ids((GLA_W, GLA_W), 0, HEAD_DIM)
               == _group_ids((GLA_W, GLA_W), 1, HEAD_DIM))
    ones256 = jnp.where(same256, 1.0, 0.0).astype(BF16)
    lane_grp = _group_ids((1, GLA_W), 1, HEAD_DIM)

    scale = HEAD_DIM ** -0.5
    q_slabs = []
    for c in range(ATTN_KV_HEADS):
        qs = p[:, OFF_AQ + c * GLA_W: OFF_AQ + (c + 1) * GLA_W]
        q_slabs.append(_group_rms(qs, qg_ref[...], ones256) * scale)
    kn = _group_rms(p[:, OFF_AK:OFF_AK + ATTN_KV_W], kg_ref[...],
                    ones256[:ATTN_KV_W, :ATTN_KV_W]).astype(BF16)
    vv = p[:, OFF_AV:OFF_AV + ATTN_KV_W].astype(BF16)
    src = _group_ids((ATTN_KV_W, 2 * GLA_W), 0, HEAD_DIM)
    dst = _group_ids((ATTN_KV_W, 2 * GLA_W), 1, GLA_W)
    lane_in = lax.broadcasted_iota(jnp.int32, (ATTN_KV_W, 2 * GLA_W), 0) % HEAD_DIM
    lane_out = lax.broadcasted_iota(jnp.int32, (ATTN_KV_W, 2 * GLA_W), 1) % HEAD_DIM
    expand = jnp.where((src == dst) & (lane_in == lane_out), 1.0, 0.0).astype(BF16)
    kbuf[WINDOW:WINDOW + tt, :] = _dot(kn, expand).astype(BF16)
    vbuf[WINDOW:WINDOW + tt, :] = _dot(vv, expand).astype(BF16)

    rows = ATTN_GROUP * WINDOW
    qi = lax.broadcasted_iota(jnp.int32, (rows, 2 * WINDOW), 0) % WINDOW
    kj = lax.broadcasted_iota(jnp.int32, (rows, 2 * WINDOW), 1)
    rel = qi + WINDOW - kj
    band = (rel >= 0) & (rel < WINDOW)
    row_head = _group_ids((rows, 1), 0, WINDOW)
    first_key = jnp.where(ti > 0, 0, WINDOW)
    nblk = tt // WINDOW
    for c in range(ATTN_KV_HEADS):
        sink_col = jnp.zeros((rows, 1), F32)
        for g in range(ATTN_GROUP):
            sink_col = jnp.where(row_head == g, sink_ref[ATTN_GROUP * c + g], sink_col)
        for j in range(nblk):
            qblk = q_slabs[c][j * WINDOW:(j + 1) * WINDOW, :]
            qstack = jnp.concatenate(
                [jnp.where(lane_grp == g, qblk, 0.0) for g in range(ATTN_GROUP)],
                axis=0).astype(BF16)
            k2 = kbuf[j * WINDOW:(j + 2) * WINDOW, c * GLA_W:(c + 1) * GLA_W]
            v2 = vbuf[j * WINDOW:(j + 2) * WINDOW, c * GLA_W:(c + 1) * GLA_W]
            s = _dot_nt(qstack, k2)
            mask = band & (kj >= first_key) if j == 0 else band
            s = jnp.where(mask, s, NEG_INF)
            m = jnp.maximum(jnp.max(s, axis=-1, keepdims=True), sink_col)
            e = jnp.exp(s - m)
            den = jnp.sum(e, axis=-1, keepdims=True) + jnp.exp(sink_col - m)
            o = _dot(e.astype(BF16), v2) / den
            out = jnp.zeros((WINDOW, GLA_W), F32)
            for g in range(ATTN_GROUP):
                out = jnp.where(lane_grp == g, o[g * WINDOW:(g + 1) * WINDOW, :], out)
            mixed[j * WINDOW:(j + 1) * WINDOW, c * GLA_W:(c + 1) * GLA_W] = out.astype(BF16)

    u = _gelu_tanh(p[:, OFF_SU:OFF_SU + SGU_W])
    vn = _group_rms(_gelu_tanh(p[:, OFF_SV:OFF_SV + SGU_W]), svg_ref[...], ones256)
    wt = lax.broadcasted_iota(jnp.int32, (SGU_CHUNK, SGU_GROUPS * SGU_CHUNK), 0)
    ws = lax.broadcasted_iota(jnp.int32, (SGU_CHUNK, SGU_GROUPS * SGU_CHUNK), 1) % SGU_CHUNK
    w_causal = jnp.where(ws <= wt, sgu_w_ref[...], 0.0).astype(BF16)
    for j in range(tt // SGU_CHUNK):
        vchunk = vn[j * SGU_CHUNK:(j + 1) * SGU_CHUNK, :]
        vblk = jnp.concatenate(
            [jnp.where(lane_grp == g, vchunk, 0.0) for g in range(SGU_GROUPS)],
            axis=0).astype(BF16)
        sg = _dot(w_causal, vblk) + sgu_b_ref[...]
        ob = u[j * SGU_CHUNK:(j + 1) * SGU_CHUNK, :] * sg
        mixed[j * SGU_CHUNK:(j + 1) * SGU_CHUNK, ATTN_Q_W:ATTN_Q_W + SGU_W] = ob.astype(BF16)

    logits = _dot(lr.astype(BF16), gup_ref[...]) + gb_ref[...]
    g_log = -(jnp.maximum(-logits, 0.0) + jnp.log1p(jnp.exp(-jnp.abs(logits)))) * (1.0 / GLA_TAU)
    g_hi = g_log.astype(BF16)
    g_lo = (g_log - g_hi.astype(F32)).astype(BF16)
    cq = p[:, OFF_CQ:OFF_CQ + GLA_W] * scale
    ck = p[:, OFF_CK:OFF_CK + GLA_W]
    cv = p[:, OFF_CV:OFF_CV + GLA_W]
    cgate = p[:, OFF_CG:OFF_CG + GLA_W]
    C = GLA_CHUNK
    tri = jnp.where(lax.broadcasted_iota(jnp.int32, (C, C), 1)
                    <= lax.broadcasted_iota(jnp.int32, (C, C), 0), 1.0, 0.0).astype(BF16)
    hrows = GLA_HEADS * C
    causal = (lax.broadcasted_iota(jnp.int32, (hrows, C), 1)
              <= lax.broadcasted_iota(jnp.int32, (hrows, C), 0) % C)
    st = state[...]
    o_chunks = []
    for n in range(tt // C):
        r = slice(n * C, (n + 1) * C)
        b = _dot(tri, g_hi[r, :]) + _dot(tri, g_lo[r, :])
        b_last = b[C - 1:C, :]
        qd = (cq[r, :] * jnp.exp(b)).astype(BF16)
        ki = (ck[r, :] * jnp.exp(-b)).astype(BF16)
        ks = (ck[r, :] * jnp.exp(b_last - b)).astype(BF16)
        vc = cv[r, :].astype(BF16)
        qd_f = qd.astype(F32)
        qexp = jnp.concatenate(
            [jnp.where(lane_grp == hh, qd_f, 0.0) for hh in range(GLA_HEADS)],
            axis=0).astype(BF16)
        attn = jnp.where(causal, _dot_nt(qexp, ki), 0.0).astype(BF16)
        rfull = _dot(attn, vc)
        o_intra = jnp.zeros((C, GLA_W), F32)
        for hh in range(GLA_HEADS):
            o_intra = jnp.where(lane_grp == hh, rfull[hh * C:(hh + 1) * C, :], o_intra)
        o_inter = _dot_nt(qd, st.astype(BF16))
        o_chunks.append(o_intra + o_inter)
        delta_t = _dot(cv[r, :].T.astype(BF16), ks)
        st = st * jnp.exp(b_last) + jnp.where(same256, delta_t, 0.0)
    state[...] = st
    oc = jnp.concatenate(o_chunks, axis=0)
    oc = _group_rms(oc, og_ref[...], ones256) * (cgate * _sigmoid(cgate))
    mixed[:, ATTN_Q_W + SGU_W:MIX_W] = oc.astype(BF16)

    o_ref[...] = x + _dot(mixed[...], w_out_ref[...])


def _mixer(x, gain, w_in, w_lr, qg, kg, sinks, svg, sgu_w, sgu_b, gup, gb, og, w_out):
    B, T, _ = x.shape
    tt = MIX_TILE_T
    vmem = [
        _resident((1, D_MODEL)),
        _resident((D_MODEL, IN_MAIN_W)),
        _resident((D_MODEL, GLA_GATE_RANK)),
        _resident((1, GLA_W)),
        _resident((1, ATTN_KV_W)),
        pl.BlockSpec(memory_space=pltpu.SMEM),
        _resident((1, SGU_W)),
        _resident((SGU_CHUNK, SGU_GROUPS * SGU_CHUNK)),
        _resident((SGU_CHUNK, SGU_W)),
        _resident((GLA_GATE_RANK, GLA_W)),
        _resident((1, GLA_W)),
        _resident((1, GLA_W)),
        _resident((MIX_W, D_MODEL)),
    ]
    return pl.pallas_call(
        _mixer_kernel,
        out_shape=jax.ShapeDtypeStruct((B, T, D_MODEL), F32),
        grid=(B, T // tt),
        in_specs=[pl.BlockSpec((None, tt, D_MODEL), lambda b, t: (b, t, 0))] + vmem,
        out_specs=pl.BlockSpec((None, tt, D_MODEL), lambda b, t: (b, t, 0)),
        scratch_shapes=[
            pltpu.VMEM((WINDOW + tt, 2 * GLA_W), BF16),
            pltpu.VMEM((WINDOW + tt, 2 * GLA_W), BF16),
            pltpu.VMEM((GLA_W, GLA_W), F32),
            pltpu.VMEM((tt, MIX_W), BF16),
        ],
        compiler_params=pltpu.CompilerParams(
            dimension_semantics=("parallel", "arbitrary"), vmem_limit_bytes=VMEM_LIMIT_BYTES),
        name="token_mixer",
    )(x, gain, w_in, w_lr, qg, kg, sinks, svg, sgu_w, sgu_b, gup, gb, og, w_out)


def kernel(x, ffn1_norm, ffn1_w_gate, ffn1_w_up, ffn1_w_down, mix_norm, w_in, attn_q_norm,
           attn_k_norm, attn_sinks, sgu_v_norm, sgu_w, sgu_b, gla_w_gate_up, gla_b_gate,
           gla_out_norm, w_out, ffn2_norm, ffn2_w_gate, ffn2_w_up, ffn2_w_down):
    B, T, D = x.shape
    row = lambda v: v.reshape(1, -1).astype(F32)
    for l in range(DEPTH):
        x = _ffn(x.reshape(B * T, D), row(ffn1_norm[l]), ffn1_w_gate[l].astype(BF16),
                 ffn1_w_up[l].astype(BF16), ffn1_w_down[l].astype(BF16)).reshape(B, T, D)
        x = _mixer(
            x, row(mix_norm[l]),
            w_in[l, :, :IN_MAIN_W].astype(BF16), w_in[l, :, IN_MAIN_W:].astype(BF16),
            row(jnp.tile(attn_q_norm[l], ATTN_GROUP)), row(jnp.tile(attn_k_norm[l], ATTN_KV_HEADS)),
            attn_sinks[l].astype(F32), row(sgu_v_norm[l]),
            jnp.transpose(sgu_w[l], (1, 0, 2)).reshape(SGU_CHUNK, SGU_GROUPS * SGU_CHUNK).astype(F32),
            jnp.repeat(sgu_b[l].T, HEAD_DIM, axis=1).astype(F32),
            gla_w_gate_up[l].astype(BF16), row(gla_b_gate[l]),
            row(jnp.tile(gla_out_norm[l], GLA_HEADS)), w_out[l].astype(BF16))
        x = _ffn(x.reshape(B * T, D), row(ffn2_norm[l]), ffn2_w_gate[l].astype(BF16),
                 ffn2_w_up[l].astype(BF16), ffn2_w_down[l].astype(BF16)).reshape(B, T, D)
    return x
```

```python
import functools

import jax
import jax.numpy as jnp
from jax import lax
from jax.experimental import pallas as pl
from jax.experimental.pallas import tpu as pltpu

F32 = jnp.float32
BF16 = jnp.bfloat16

D_MODEL = 1024
D_FF = 2816
DEPTH = 2
HEAD_DIM = 64
ATTN_Q_HEADS = 8
ATTN_KV_HEADS = 2
ATTN_GROUP = ATTN_Q_HEADS // ATTN_KV_HEADS
WINDOW = 128
SGU_GROUPS = 4
SGU_CHUNK = 128
GLA_HEADS = 4
GLA_GATE_RANK = 16
GLA_TAU = 16.0
GLA_CHUNK = 64
RMS_EPS = 1e-6
NEG_INF = -1e30

ATTN_Q_W = ATTN_Q_HEADS * HEAD_DIM
ATTN_KV_W = ATTN_KV_HEADS * HEAD_DIM
SGU_W = SGU_GROUPS * HEAD_DIM
GLA_W = GLA_HEADS * HEAD_DIM
OFF_AQ = 0
OFF_AK = OFF_AQ + ATTN_Q_W
OFF_AV = OFF_AK + ATTN_KV_W
OFF_SU = OFF_AV + ATTN_KV_W
OFF_SV = OFF_SU + SGU_W
OFF_CQ = OFF_SV + SGU_W
OFF_CK = OFF_CQ + GLA_W
OFF_CV = OFF_CK + GLA_W
OFF_CG = OFF_CV + GLA_W
OFF_LR = OFF_CG + GLA_W
IN_MAIN_W = OFF_LR
MIX_W = ATTN_Q_W + SGU_W + GLA_W

FFN_TILE_M = 512
MXU_WIDTH = 256
FFN_CHUNK_BOUNDS = ((0, 6 * MXU_WIDTH), (6 * MXU_WIDTH, D_FF))
MIX_TILE_T = 512
VMEM_LIMIT_BYTES = 56 * 1024 * 1024


def _dot(a, b):
    return jnp.dot(a, b, preferred_element_type=F32)


def _dot_nt(a, b):
    return lax.dot_general(a, b, (((1,), (1,)), ((), ())), preferred_element_type=F32)


def _sigmoid(x):
    return 1.0 / (1.0 + jnp.exp(-x))


def _gelu_tanh(x):
    c = 0.7978845608028654
    return x * (0.5 * (1.0 + jnp.tanh(c * (x + 0.044715 * (x * x * x)))))


def _rms_rows(x, gain):
    ms = jnp.mean(x * x, axis=-1, keepdims=True)
    return x * lax.rsqrt(ms + RMS_EPS) * gain


def _group_ids(shape, dim, group):
    return lax.broadcasted_iota(jnp.int32, shape, dim) // group


def _group_rms(x, gain, same_group):
    ss = _dot((x * x).astype(BF16), same_group)
    return x * lax.rsqrt(ss * (1.0 / HEAD_DIM) + RMS_EPS) * gain


def _ffn_kernel(x_ref, gain_ref, wg_ref, wu_ref, wd_ref, o_ref):
    x = x_ref[...]
    h = _rms_rows(x, gain_ref[...]).astype(BF16)
    y = None
    for lo, hi in FFN_CHUNK_BOUNDS:
        sl = slice(lo, hi)
        gate = _dot(h, wg_ref[:, sl])
        up = _dot(h, wu_ref[:, sl])
        act = (gate * _sigmoid(gate) * up).astype(BF16)
        part = _dot(act, wd_ref[sl, :])
        y = part if y is None else y + part
    o_ref[...] = x + 0.5 * y


def _resident(shape):
    nd = len(shape)
    return pl.BlockSpec(shape, lambda *_: (0,) * nd, pipeline_mode=pl.Buffered(1))


def _ffn(x2d, gain, wg, wu, wd):
    m = x2d.shape[0]
    return pl.pallas_call(
        _ffn_kernel,
        out_shape=jax.ShapeDtypeStruct((m, D_MODEL), F32),
        grid=(m // FFN_TILE_M,),
        in_specs=[
            pl.BlockSpec((FFN_TILE_M, D_MODEL), lambda i: (i, 0)),
            _resident((1, D_MODEL)),
            _resident((D_MODEL, D_FF)),
            _resident((D_MODEL, D_FF)),
            _resident((D_FF, D_MODEL)),
        ],
        out_specs=pl.BlockSpec((FFN_TILE_M, D_MODEL), lambda i: (i, 0)),
        compiler_params=pltpu.CompilerParams(
            dimension_semantics=("parallel",), vmem_limit_bytes=VMEM_LIMIT_BYTES),
        name="swiglu_ffn",
    )(x2d, gain, wg, wu, wd)


def _mixer_kernel(x_ref, gain_ref, w_in_ref, w_lr_ref, qg_ref, kg_ref, sink_ref, svg_ref,
                  sgu_w_ref, sgu_b_ref, gup_ref, gb_ref, og_ref, w_out_ref,
                  o_ref, kbuf, vbuf, state, mixed):
    tt = MIX_TILE_T
    ti = pl.program_id(1)

    @pl.when(ti > 0)
    def _():
        kbuf[0:WINDOW, :] = kbuf[tt:tt + WINDOW, :]
        vbuf[0:WINDOW, :] = vbuf[tt:tt + WINDOW, :]

    @pl.when(ti == 0)
    def _():
        kbuf[0:WINDOW, :] = jnp.zeros((WINDOW, 2 * GLA_W), BF16)
        vbuf[0:WINDOW, :] = jnp.zeros((WINDOW, 2 * GLA_W), BF16)
        state[...] = jnp.zeros_like(state)

    x = x_ref[...]
    h = _rms_rows(x, gain_ref[...]).astype(BF16)
    p = _dot(h, w_in_ref[...])
    lr = _dot(h, w_lr_ref[...])

    same256 = (_group_ids((GLA_W, GLA_W), 0, HEAD_DIM)
               == _group_ids((GLA_W, GLA_W), 1, HEAD_DIM))
    ones256 = jnp.where(same256, 1.0, 0.0).astype(BF16)
    lane_grp = _group_ids((1, GLA_W), 1, HEAD_DIM)

    scale = HEAD_DIM ** -0.5
    q_slabs = []
    for c in range(ATTN_KV_HEADS):
        qs = p[:, OFF_AQ + c * GLA_W: OFF_AQ + (c + 1) * GLA_W]
        q_slabs.append(_group_rms(qs, qg_ref[...], ones256) * scale)
    kn = _group_rms(p[:, OFF_AK:OFF_AK + ATTN_KV_W], kg_ref[...],
                    ones256[:ATTN_KV_W, :ATTN_KV_W]).astype(BF16)
    vv = p[:, OFF_AV:OFF_AV + ATTN_KV_W].astype(BF16)
    src = _group_ids((ATTN_KV_W, 2 * GLA_W), 0, HEAD_DIM)
    dst = _group_ids((ATTN_KV_W, 2 * GLA_W), 1, GLA_W)
    lane_in = lax.broadcasted_iota(jnp.int32, (ATTN_KV_W, 2 * GLA_W), 0) % HEAD_DIM
    lane_out = lax.broadcasted_iota(jnp.int32, (ATTN_KV_W, 2 * GLA_W), 1) % HEAD_DIM
    expand = jnp.where((src == dst) & (lane_in == lane_out), 1.0, 0.0).astype(BF16)
    kbuf[WINDOW:WINDOW + tt, :] = _dot(kn, expand).astype(BF16)
    vbuf[WINDOW:WINDOW + tt, :] = _dot(vv, expand).astype(BF16)

    rows = ATTN_GROUP * WINDOW
    qi = lax.broadcasted_iota(jnp.int32, (rows, 2 * WINDOW), 0) % WINDOW
    kj = lax.broadcasted_iota(jnp.int32, (rows, 2 * WINDOW), 1)
    rel = qi + WINDOW - kj
    band = (rel >= 0) & (rel < WINDOW)
    row_head = _group_ids((rows, 1), 0, WINDOW)
    first_key = jnp.where(ti > 0, 0, WINDOW)
    nblk = tt // WINDOW
    for c in range(ATTN_KV_HEADS):
        sink_col = jnp.zeros((rows, 1), F32)
        for g in range(ATTN_GROUP):
            sink_col = jnp.where(row_head == g, sink_ref[ATTN_GROUP * c + g], sink_col)
        for j in range(nblk):
            qblk = q_slabs[c][j * WINDOW:(j + 1) * WINDOW, :]
            qstack = jnp.concatenate(
                [jnp.where(lane_grp == g, qblk, 0.0) for g in range(ATTN_GROUP)],
                axis=0).astype(BF16)
            k2 = kbuf[j * WINDOW:(j + 2) * WINDOW, c * GLA_W:(c + 1) * GLA_W]
            v2 = vbuf[j * WINDOW:(j + 2) * WINDOW, c * GLA_W:(c + 1) * GLA_W]
            s = _dot_nt(qstack, k2)
            mask = band & (kj >= first_key) if j == 0 else band
            s = jnp.where(mask, s, NEG_INF)
            m = jnp.maximum(jnp.max(s, axis=-1, keepdims=True), sink_col)
            e = jnp.exp(s - m)
            den = jnp.sum(e, axis=-1, keepdims=True) + jnp.exp(sink_col - m)
            o = _dot(e.astype(BF16), v2) / den
            out = jnp.zeros((WINDOW, GLA_W), F32)
            for g in range(ATTN_GROUP):
                out = jnp.where(lane_grp == g, o[g * WINDOW:(g + 1) * WINDOW, :], out)
            mixed[j * WINDOW:(j + 1) * WINDOW, c * GLA_W:(c + 1) * GLA_W] = out.astype(BF16)

    u = _gelu_tanh(p[:, OFF_SU:OFF_SU + SGU_W])
    vn = _group_rms(_gelu_tanh(p[:, OFF_SV:OFF_SV + SGU_W]), svg_ref[...], ones256)
    wt = lax.broadcasted_iota(jnp.int32, (SGU_CHUNK, SGU_GROUPS * SGU_CHUNK), 0)
    ws = lax.broadcasted_iota(jnp.int32, (SGU_CHUNK, SGU_GROUPS * SGU_CHUNK), 1) % SGU_CHUNK
    w_causal = jnp.where(ws <= wt, sgu_w_ref[...], 0.0).astype(BF16)
    for j in range(tt // SGU_CHUNK):
        vchunk = vn[j * SGU_CHUNK:(j + 1) * SGU_CHUNK, :]
        vblk = jnp.concatenate(
            [jnp.where(lane_grp == g, vchunk, 0.0) for g in range(SGU_GROUPS)],
            axis=0).astype(BF16)
        sg = _dot(w_causal, vblk) + sgu_b_ref[...]
        ob = u[j * SGU_CHUNK:(j + 1) * SGU_CHUNK, :] * sg
        mixed[j * SGU_CHUNK:(j + 1) * SGU_CHUNK, ATTN_Q_W:ATTN_Q_W + SGU_W] = ob.astype(BF16)

    logits = _dot(lr.astype(BF16), gup_ref[...]) + gb_ref[...]
    g_log = -(jnp.maximum(-logits, 0.0) + jnp.log1p(jnp.exp(-jnp.abs(logits)))) * (1.0 / GLA_TAU)
    g_hi = g_log.astype(BF16)
    g_lo = (g_log - g_hi.astype(F32)).astype(BF16)
    cq = p[:, OFF_CQ:OFF_CQ + GLA_W] * scale
    ck = p[:, OFF_CK:OFF_CK + GLA_W]
    cv = p[:, OFF_CV:OFF_CV + GLA_W]
    cgate = p[:, OFF_CG:OFF_CG + GLA_W]
    C = GLA_CHUNK
    tri = jnp.where(lax.broadcasted_iota(jnp.int32, (C, C), 1)
                    <= lax.broadcasted_iota(jnp.int32, (C, C), 0), 1.0, 0.0).astype(BF16)
    hrows = GLA_HEADS * C
    causal = (lax.broadcasted_iota(jnp.int32, (hrows, C), 1)
              <= lax.broadcasted_iota(jnp.int32, (hrows, C), 0) % C)
    st = state[...]
    o_chunks = []
    for n in range(tt // C):
        r = slice(n * C, (n + 1) * C)
        b = _dot(tri, g_hi[r, :]) + _dot(tri, g_lo[r, :])
        b_last = b[C - 1:C, :]
        qd = (cq[r, :] * jnp.exp(b)).astype(BF16)
        ki = (ck[r, :] * jnp.exp(-b)).astype(BF16)
        ks = (ck[r, :] * jnp.exp(b_last - b)).astype(BF16)
        vc = cv[r, :].astype(BF16)
        qd_f = qd.astype(F32)
        qexp = jnp.concatenate(
            [jnp.where(lane_grp == hh, qd_f, 0.0) for hh in range(GLA_HEADS)],
            axis=0).astype(BF16)
        attn = jnp.where(causal, _dot_nt(qexp, ki), 0.0).astype(BF16)
        rfull = _dot(attn, vc)
        o_intra = jnp.zeros((C, GLA_W), F32)
        for hh in range(GLA_HEADS):
            o_intra = jnp.where(lane_grp == hh, rfull[hh * C:(hh + 1) * C, :], o_intra)
        o_inter = _dot_nt(qd, st.astype(BF16))
        o_chunks.append(o_intra + o_inter)
        delta_t = _dot(cv[r, :].T.astype(BF16), ks)
        st = st * jnp.exp(b_last) + jnp.where(same256, delta_t, 0.0)
    state[...] = st
    oc = jnp.concatenate(o_chunks, axis=0)
    oc = _group_rms(oc, og_ref[...], ones256) * (cgate * _sigmoid(cgate))
    mixed[:, ATTN_Q_W + SGU_W:MIX_W] = oc.astype(BF16)

    o_ref[...] = x + _dot(mixed[...], w_out_ref[...])


def _mixer(x, gain, w_in, w_lr, qg, kg, sinks, svg, sgu_w, sgu_b, gup, gb, og, w_out):
    B, T, _ = x.shape
    tt = MIX_TILE_T
    vmem = [
        _resident((1, D_MODEL)),
        _resident((D_MODEL, IN_MAIN_W)),
        _resident((D_MODEL, GLA_GATE_RANK)),
        _resident((1, GLA_W)),
        _resident((1, ATTN_KV_W)),
        pl.BlockSpec(memory_space=pltpu.SMEM),
        _resident((1, SGU_W)),
        _resident((SGU_CHUNK, SGU_GROUPS * SGU_CHUNK)),
        _resident((SGU_CHUNK, SGU_W)),
        _resident((GLA_GATE_RANK, GLA_W)),
        _resident((1, GLA_W)),
        _resident((1, GLA_W)),
        _resident((MIX_W, D_MODEL)),
    ]
    return pl.pallas_call(
        _mixer_kernel,
        out_shape=jax.ShapeDtypeStruct((B, T, D_MODEL), F32),
        grid=(B, T // tt),
        in_specs=[pl.BlockSpec((None, tt, D_MODEL), lambda b, t: (b, t, 0))] + vmem,
        out_specs=pl.BlockSpec((None, tt, D_MODEL), lambda b, t: (b, t, 0)),
        scratch_shapes=[
            pltpu.VMEM((WINDOW + tt, 2 * GLA_W), BF16),
            pltpu.VMEM((WINDOW + tt, 2 * GLA_W), BF16),
            pltpu.VMEM((GLA_W, GLA_W), F32),
            pltpu.VMEM((tt, MIX_W), BF16),
        ],
        compiler_params=pltpu.CompilerParams(
            dimension_semantics=("parallel", "arbitrary"), vmem_limit_bytes=VMEM_LIMIT_BYTES),
        name="token_mixer",
    )(x, gain, w_in, w_lr, qg, kg, sinks, svg, sgu_w, sgu_b, gup, gb, og, w_out)


def kernel(x, ffn1_norm, ffn1_w_gate, ffn1_w_up, ffn1_w_down, mix_norm, w_in, attn_q_norm,
           attn_k_norm, attn_sinks, sgu_v_norm, sgu_w, sgu_b, gla_w_gate_up, gla_b_gate,
           gla_out_norm, w_out, ffn2_norm, ffn2_w_gate, ffn2_w_up, ffn2_w_down):
    B, T, D = x.shape
    row = lambda v: v.reshape(1, -1).astype(F32)
    for l in range(DEPTH):
        x = _ffn(x.reshape(B * T, D), row(ffn1_norm[l]), ffn1_w_gate[l].astype(BF16),
                 ffn1_w_up[l].astype(BF16), ffn1_w_down[l].astype(BF16)).reshape(B, T, D)
        x = _mixer(
            x, row(mix_norm[l]),
            w_in[l, :, :IN_MAIN_W].astype(BF16), w_in[l, :, IN_MAIN_W:].astype(BF16),
            row(jnp.tile(attn_q_norm[l], ATTN_GROUP)), row(jnp.tile(attn_k_norm[l], ATTN_KV_HEADS)),
            attn_sinks[l].astype(F32), row(sgu_v_norm[l]),
            jnp.transpose(sgu_w[l], (1, 0, 2)).reshape(SGU_CHUNK, SGU_GROUPS * SGU_CHUNK).astype(F32),
            jnp.repeat(sgu_b[l].T, HEAD_DIM, axis=1).astype(F32),
            gla_w_gate_up[l].astype(BF16), row(gla_b_gate[l]),
            row(jnp.tile(gla_out_norm[l], GLA_HEADS)), w_out[l].astype(BF16))
        x = _ffn(x.reshape(B * T, D), row(ffn2_norm[l]), ffn2_w_gate[l].astype(BF16),
                 ffn2_w_up[l].astype(BF16), ffn2_w_down[l].astype(BF16)).reshape(B, T, D)
    return x
```

```python
import jax
import jax.numpy as jnp
from jax import lax
from jax.experimental import pallas as pl
from jax.experimental.pallas import tpu as pltpu

F32 = jnp.float32
BF16 = jnp.bfloat16

D_MODEL = 1024
D_FF = 2816
DEPTH = 2
HEAD_DIM = 64
ATTN_Q_HEADS = 8
ATTN_KV_HEADS = 2
ATTN_GROUP = ATTN_Q_HEADS // ATTN_KV_HEADS
WINDOW = 128
SGU_GROUPS = 4
SGU_CHUNK = 128
GLA_HEADS = 4
GLA_GATE_RANK = 16
GLA_TAU = 16.0
GLA_CHUNK = 64
RMS_EPS = 1e-6
NEG_INF = -1e30
LOG2_E = 1.4426950408889634

ATTN_Q_W = ATTN_Q_HEADS * HEAD_DIM
ATTN_KV_W = ATTN_KV_HEADS * HEAD_DIM
SGU_W = SGU_GROUPS * HEAD_DIM
GLA_W = GLA_HEADS * HEAD_DIM
OFF_AQ = 0
OFF_AK = OFF_AQ + ATTN_Q_W
OFF_AV = OFF_AK + ATTN_KV_W
OFF_SU = OFF_AV + ATTN_KV_W
OFF_SV = OFF_SU + SGU_W
OFF_CQ = OFF_SV + SGU_W
OFF_CK = OFF_CQ + GLA_W
OFF_CV = OFF_CK + GLA_W
OFF_CG = OFF_CV + GLA_W
OFF_LR = OFF_CG + GLA_W
IN_MAIN_W = OFF_LR
MIX_W = ATTN_Q_W + SGU_W + GLA_W

FFN_TILE_M = 512
MXU_WIDTH = 256
FFN_CHUNK_BOUNDS = ((0, 6 * MXU_WIDTH), (6 * MXU_WIDTH, D_FF))
MIX_TILE_T = 512
VMEM_LIMIT_BYTES = 56 * 1024 * 1024


def _dot(a, b):
    return jnp.dot(a, b, preferred_element_type=F32)


def _dot_nt(a, b):
    return lax.dot_general(a, b, (((1,), (1,)), ((), ())), preferred_element_type=F32)


def _sigmoid(x):
    return 1.0 / (1.0 + jnp.exp(-x))


def _gelu_tanh(x):
    c = 0.7978845608028654
    return x * (0.5 * (1.0 + jnp.tanh(c * (x + 0.044715 * (x * x * x)))))


def _rms_rows(x, gain):
    ms = jnp.mean(x * x, axis=-1, keepdims=True)
    return x * lax.rsqrt(ms + RMS_EPS) * gain


def _group_ids(shape, dim, group):
    return lax.broadcasted_iota(jnp.int32, shape, dim) // group


def _group_rms(x, gain, same_group):
    ss = _dot((x * x).astype(BF16), same_group)
    return x * lax.rsqrt(ss * (1.0 / HEAD_DIM) + RMS_EPS) * gain


def _ffn_kernel(x_ref, gain_ref, wg_ref, wu_ref, wd_ref, o_ref):
    x = x_ref[...]
    h = _rms_rows(x, gain_ref[...]).astype(BF16)
    y = None
    for lo, hi in FFN_CHUNK_BOUNDS:
        sl = slice(lo, hi)
        gate = _dot(h, wg_ref[:, sl])
        up = _dot(h, wu_ref[:, sl])
        act = (gate * _sigmoid(gate) * up).astype(BF16)
        part = _dot(act, wd_ref[sl, :])
        y = part if y is None else y + part
    o_ref[...] = x + 0.5 * y


def _resident(shape):
    nd = len(shape)
    return pl.BlockSpec(shape, lambda *_: (0,) * nd, pipeline_mode=pl.Buffered(1))


def _ffn(x2d, gain, wg, wu, wd):
    m = x2d.shape[0]
    return pl.pallas_call(
        _ffn_kernel,
        out_shape=jax.ShapeDtypeStruct((m, D_MODEL), F32),
        grid=(m // FFN_TILE_M,),
        in_specs=[
            pl.BlockSpec((FFN_TILE_M, D_MODEL), lambda i: (i, 0)),
            _resident((1, D_MODEL)),
            _resident((D_MODEL, D_FF)),
            _resident((D_MODEL, D_FF)),
            _resident((D_FF, D_MODEL)),
        ],
        out_specs=pl.BlockSpec((FFN_TILE_M, D_MODEL), lambda i: (i, 0)),
        compiler_params=pltpu.CompilerParams(
            dimension_semantics=("parallel",), vmem_limit_bytes=VMEM_LIMIT_BYTES),
        name="swiglu_ffn",
    )(x2d, gain, wg, wu, wd)


def _mixer_kernel(x_ref, gain_ref, w_in_ref, w_lr_ref, qg_ref, kg_ref, sink_ref, svg_ref,
                  sgu_w_ref, sgu_b_ref, gup_ref, gb_ref, og_ref, w_out_ref,
                  o_ref, kbuf, vbuf, state, band_bias):
    tt = MIX_TILE_T
    ti = pl.program_id(1)
    rows = ATTN_GROUP * WINDOW
    kj = lax.broadcasted_iota(jnp.int32, (rows, 2 * WINDOW), 1)

    @pl.when(ti > 0)
    def _():
        kbuf[0:WINDOW, :] = kbuf[tt:tt + WINDOW, :]
        vbuf[0:WINDOW, :] = vbuf[tt:tt + WINDOW, :]

    @pl.when(ti == 0)
    def _():
        kbuf[0:WINDOW, :] = jnp.zeros((WINDOW, 2 * GLA_W), BF16)
        vbuf[0:WINDOW, :] = jnp.zeros((WINDOW, 2 * GLA_W), BF16)
        state[...] = jnp.zeros_like(state)
        rel = lax.broadcasted_iota(jnp.int32, (rows, 2 * WINDOW), 0) % WINDOW + WINDOW - kj
        band_bias[...] = jnp.where((rel >= 0) & (rel < WINDOW), 0.0, NEG_INF)

    x = x_ref[...]
    h = _rms_rows(x, gain_ref[...]).astype(BF16)
    p_attn = _dot(h, w_in_ref[:, OFF_AQ:OFF_SU])
    lr = _dot(h, w_lr_ref[...])
    env = {}

    def project(name, lo):
        env[name] = _dot(h, w_in_ref[:, lo:lo + MXU_WIDTH])

    same256 = (_group_ids((GLA_W, GLA_W), 0, HEAD_DIM)
               == _group_ids((GLA_W, GLA_W), 1, HEAD_DIM))
    ones256 = jnp.where(same256, 1.0, 0.0).astype(BF16)
    lane_grp = _group_ids((1, GLA_W), 1, HEAD_DIM)
    scale = HEAD_DIM ** -0.5

    C = GLA_CHUNK
    chunks = [slice(n * C, (n + 1) * C) for n in range(tt // C)]
    logits = _dot(lr.astype(BF16), gup_ref[...]) + gb_ref[...]
    g_log = -(jnp.maximum(-logits, 0.0) + jnp.log(1.0 + jnp.exp(-jnp.abs(logits)))) * (1.0 / GLA_TAU)
    g_hi = g_log.astype(BF16)
    g_lo = (g_log - g_hi.astype(F32)).astype(BF16)

    q_slabs = []
    for c in range(ATTN_KV_HEADS):
        qs = p_attn[:, OFF_AQ + c * GLA_W: OFF_AQ + (c + 1) * GLA_W]
        q_slabs.append(_group_rms(qs, qg_ref[...], ones256) * (scale * LOG2_E))
    kn = _group_rms(p_attn[:, OFF_AK:OFF_AK + ATTN_KV_W], kg_ref[...],
                    ones256[:ATTN_KV_W, :ATTN_KV_W])

    project("su", OFF_SU)
    project("sv", OFF_SV)

    tri = jnp.where(lax.broadcasted_iota(jnp.int32, (C, C), 1)
                    <= lax.broadcasted_iota(jnp.int32, (C, C), 0), 1.0, 0.0).astype(BF16)
    b = jnp.concatenate([_dot(tri, g_hi[r, :]) + _dot(tri, g_lo[r, :]) for r in chunks], axis=0)
    b_last = [b[r.stop - 1:r.stop, :] for r in chunks]
    b_end = jnp.concatenate([jnp.broadcast_to(bl, (C, GLA_W)) for bl in b_last], axis=0)
    decay_q = jnp.exp(b)
    decay_k_intra = jnp.exp(-b)
    decay_k_state = jnp.exp(b_end - b)
    decay_state = [jnp.exp(bl) for bl in b_last]

    low_half = lax.broadcasted_iota(jnp.int32, (1, ATTN_KV_W), 1) < HEAD_DIM

    def replicate(a):
        swapped = pltpu.roll(a, HEAD_DIM, axis=1)
        h0 = jnp.where(low_half, a, swapped).astype(BF16)
        h1 = jnp.where(low_half, swapped, a).astype(BF16)
        return jnp.concatenate([h0, h0, h1, h1], axis=1)

    kbuf[WINDOW:WINDOW + tt, :] = replicate(kn)
    vbuf[WINDOW:WINDOW + tt, :] = replicate(p_attn[:, OFF_AV:OFF_AV + ATTN_KV_W])

    row_head = _group_ids((rows, 1), 0, WINDOW)
    sink_cols = []
    for c in range(ATTN_KV_HEADS):
        sink_col = jnp.zeros((rows, 1), F32)
        for g in range(ATTN_GROUP):
            sink_col = jnp.where(row_head == g, sink_ref[ATTN_GROUP * c + g] * LOG2_E, sink_col)
        sink_cols.append(sink_col)
    first_key = jnp.where(ti > 0, 0, WINDOW)

    def scores(c, j):
        qblk = q_slabs[c][j * WINDOW:(j + 1) * WINDOW, :]
        qstack = jnp.concatenate(
            [jnp.where(lane_grp == g, qblk, 0.0) for g in range(ATTN_GROUP)],
            axis=0).astype(BF16)
        k2 = kbuf[j * WINDOW:(j + 2) * WINDOW, c * GLA_W:(c + 1) * GLA_W]
        s = _dot_nt(qstack, k2) + band_bias[...]
        if j == 0:
            s = jnp.where(kj >= first_key, s, NEG_INF)
        return s

    def attend(c, j, s):
        m = jnp.maximum(jnp.max(s, axis=-1, keepdims=True), sink_cols[c])
        e = jnp.exp2(s - m)
        den = jnp.sum(e, axis=-1, keepdims=True) + jnp.exp2(sink_cols[c] - m)
        v2 = vbuf[j * WINDOW:(j + 2) * WINDOW, c * GLA_W:(c + 1) * GLA_W]
        o = _dot(e.astype(BF16), v2) * (1.0 / den)
        out = jnp.zeros((WINDOW, GLA_W), F32)
        for g in range(ATTN_GROUP):
            out = jnp.where(lane_grp == g, o[g * WINDOW:(g + 1) * WINDOW, :], out)
        return out.astype(BF16)

    def gating_u():
        env["u"] = _gelu_tanh(env["su"])

    def gating_v():
        vn = _group_rms(_gelu_tanh(env["sv"]), svg_ref[...], ones256)
        wt = lax.broadcasted_iota(jnp.int32, (SGU_CHUNK, SGU_GROUPS * SGU_CHUNK), 0)
        ws = lax.broadcasted_iota(jnp.int32, (SGU_CHUNK, SGU_GROUPS * SGU_CHUNK), 1) % SGU_CHUNK
        w_causal = jnp.where(ws <= wt, sgu_w_ref[...], 0.0).astype(BF16)
        out_b = []
        for j in range(tt // SGU_CHUNK):
            vchunk = vn[j * SGU_CHUNK:(j + 1) * SGU_CHUNK, :]
            vblk = jnp.concatenate(
                [jnp.where(lane_grp == g, vchunk, 0.0) for g in range(SGU_GROUPS)],
                axis=0).astype(BF16)
            sg = _dot(w_causal, vblk) + sgu_b_ref[...]
            out_b.append((env["u"][j * SGU_CHUNK:(j + 1) * SGU_CHUNK, :] * sg).astype(BF16))
        env["out_b"] = jnp.concatenate(out_b, axis=0)

    def gla_q():
        qd_f = (env["cq"] * scale) * decay_q
        env["qd"] = qd_f.astype(BF16)
        env["qexp"] = [
            jnp.concatenate([jnp.where(lane_grp == hh, qd_f[r, :], 0.0) for hh in range(GLA_HEADS)],
                            axis=0).astype(BF16) for r in chunks]

    def gla_k():
        ki = (env["ck"] * decay_k_intra).astype(BF16)
        env["ks"] = (env["ck"] * decay_k_state).astype(BF16)
        env["attn"] = [_dot_nt(env["qexp"][n], ki[r, :]) for n, r in enumerate(chunks)]

    def gla_v():
        cv = env["cv"]
        env["vc"] = cv.astype(BF16)
        env["delta"] = [
            jnp.where(same256, _dot(cv[r, :].T.astype(BF16), env["ks"][r, :]), 0.0)
            for r in chunks]

    def gla_intra():
        hrows = GLA_HEADS * C
        causal = (lax.broadcasted_iota(jnp.int32, (hrows, C), 1)
                  <= lax.broadcasted_iota(jnp.int32, (hrows, C), 0) % C)
        o_intra = []
        for n, r in enumerate(chunks):
            rfull = _dot(jnp.where(causal, env["attn"][n], 0.0).astype(BF16), env["vc"][r, :])
            oi = jnp.zeros((C, GLA_W), F32)
            for hh in range(GLA_HEADS):
                oi = jnp.where(lane_grp == hh, rfull[hh * C:(hh + 1) * C, :], oi)
            o_intra.append(oi)
        env["o_intra"] = o_intra

    def gla_recurrence():
        st = state[...]
        o_chunks = []
        for n, r in enumerate(chunks):
            o_chunks.append(env["o_intra"][n] + _dot_nt(env["qd"][r, :], st.astype(BF16)))
            st = st * decay_state[n] + env["delta"][n]
        state[...] = st
        env["oc"] = jnp.concatenate(o_chunks, axis=0)

    order = [(c, j) for c in range(ATTN_KV_HEADS) for j in range(tt // WINDOW)]
    blocks = [[] for _ in range(ATTN_KV_HEADS)]

    def out_proj(name, piece, lo):
        env[name] = _dot(piece, w_out_ref[lo:lo + MXU_WIDTH, :])

    between_blocks = [
        [lambda: project("cq", OFF_CQ), gating_u],
        [lambda: project("ck", OFF_CK), gating_v],
        [lambda: project("cv", OFF_CV), gla_q],
        [lambda: project("cg", OFF_CG), gla_k],
        [gla_v, lambda: out_proj("y_b", env["out_b"], ATTN_Q_W)],
        [gla_intra, lambda: out_proj("y_a0", jnp.concatenate(blocks[0], axis=0), 0)],
        [gla_recurrence],
        [],
    ]
    s_next = scores(*order[0])
    for i, (c, j) in enumerate(order):
        s_cur = s_next
        if i + 1 < len(order):
            s_next = scores(*order[i + 1])
        for work in between_blocks[i]:
            work()
        blocks[c].append(attend(c, j, s_cur))
    out_proj("y_a1", jnp.concatenate(blocks[1], axis=0), GLA_W)
    cgate = env["cg"]
    oc = _group_rms(env["oc"], og_ref[...], ones256) * (cgate * _sigmoid(cgate))
    out_proj("y_c", oc.astype(BF16), ATTN_Q_W + SGU_W)
    o_ref[...] = x + env["y_a0"] + env["y_b"] + env["y_a1"] + env["y_c"]


def _mixer(x, gain, w_in, w_lr, qg, kg, sinks, svg, sgu_w, sgu_b, gup, gb, og, w_out):
    B, T, _ = x.shape
    tt = MIX_TILE_T
    vmem = [
        _resident((1, D_MODEL)),
        _resident((D_MODEL, IN_MAIN_W)),
        _resident((D_MODEL, GLA_GATE_RANK)),
        _resident((1, GLA_W)),
        _resident((1, ATTN_KV_W)),
        pl.BlockSpec(memory_space=pltpu.SMEM),
        _resident((1, SGU_W)),
        _resident((SGU_CHUNK, SGU_GROUPS * SGU_CHUNK)),
        _resident((SGU_CHUNK, SGU_W)),
        _resident((GLA_GATE_RANK, GLA_W)),
        _resident((1, GLA_W)),
        _resident((1, GLA_W)),
        _resident((MIX_W, D_MODEL)),
    ]
    return pl.pallas_call(
        _mixer_kernel,
        out_shape=jax.ShapeDtypeStruct((B, T, D_MODEL), F32),
        grid=(B, T // tt),
        in_specs=[pl.BlockSpec((None, tt, D_MODEL), lambda b, t: (b, t, 0))] + vmem,
        out_specs=pl.BlockSpec((None, tt, D_MODEL), lambda b, t: (b, t, 0)),
        scratch_shapes=[
            pltpu.VMEM((WINDOW + tt, 2 * GLA_W), BF16),
            pltpu.VMEM((WINDOW + tt, 2 * GLA_W), BF16),
            pltpu.VMEM((GLA_W, GLA_W), F32),
            pltpu.VMEM((ATTN_GROUP * WINDOW, 2 * WINDOW), F32),
        ],
        compiler_params=pltpu.CompilerParams(
            dimension_semantics=("parallel", "arbitrary"), vmem_limit_bytes=VMEM_LIMIT_BYTES),
        name="token_mixer",
    )(x, gain, w_in, w_lr, qg, kg, sinks, svg, sgu_w, sgu_b, gup, gb, og, w_out)


def kernel(x, ffn1_norm, ffn1_w_gate, ffn1_w_up, ffn1_w_down, mix_norm, w_in, attn_q_norm,
           attn_k_norm, attn_sinks, sgu_v_norm, sgu_w, sgu_b, gla_w_gate_up, gla_b_gate,
           gla_out_norm, w_out, ffn2_norm, ffn2_w_gate, ffn2_w_up, ffn2_w_down):
    B, T, D = x.shape
    row = lambda v: v.reshape(1, -1).astype(F32)
    for l in range(DEPTH):
        x = _ffn(x.reshape(B * T, D), row(ffn1_norm[l]), ffn1_w_gate[l].astype(BF16),
                 ffn1_w_up[l].astype(BF16), ffn1_w_down[l].astype(BF16)).reshape(B, T, D)
        x = _mixer(
            x, row(mix_norm[l]),
            w_in[l, :, :IN_MAIN_W].astype(BF16), w_in[l, :, IN_MAIN_W:].astype(BF16),
            row(jnp.tile(attn_q_norm[l], ATTN_GROUP)), row(jnp.tile(attn_k_norm[l], ATTN_KV_HEADS)),
            attn_sinks[l].astype(F32), row(sgu_v_norm[l]),
            jnp.transpose(sgu_w[l], (1, 0, 2)).reshape(SGU_CHUNK, SGU_GROUPS * SGU_CHUNK).astype(F32),
            jnp.repeat(sgu_b[l].T, HEAD_DIM, axis=1).astype(F32),
            gla_w_gate_up[l].astype(BF16), row(gla_b_gate[l]),
            row(jnp.tile(gla_out_norm[l], GLA_HEADS)), w_out[l].astype(BF16))
        x = _ffn(x.reshape(B * T, D), row(ffn2_norm[l]), ffn2_w_gate[l].astype(BF16),
                 ffn2_w_up[l].astype(BF16), ffn2_w_down[l].astype(BF16)).reshape(B, T, D)
    return x
```

```python
import jax
import jax.numpy as jnp
from jax import lax
from jax.experimental import pallas as pl
from jax.experimental.pallas import tpu as pltpu

F32 = jnp.float32
BF16 = jnp.bfloat16

D_MODEL = 1024
D_FF = 2816
SEQ = 4096
DEPTH = 2
HEAD_DIM = 64
ATTN_Q_HEADS = 8
ATTN_KV_HEADS = 2
ATTN_GROUP = ATTN_Q_HEADS // ATTN_KV_HEADS
WINDOW = 128
SGU_GROUPS = 4
SGU_CHUNK = 128
GLA_HEADS = 4
GLA_GATE_RANK = 16
GLA_TAU = 16.0
GLA_CHUNK = 64
RMS_EPS = 1e-6
NEG_INF = -1e30
LOG2_E = 1.4426950408889634

ATTN_Q_W = ATTN_Q_HEADS * HEAD_DIM
ATTN_KV_W = ATTN_KV_HEADS * HEAD_DIM
SGU_W = SGU_GROUPS * HEAD_DIM
GLA_W = GLA_HEADS * HEAD_DIM
OFF_AQ = 0
OFF_AK = OFF_AQ + ATTN_Q_W
OFF_AV = OFF_AK + ATTN_KV_W
OFF_SU = OFF_AV + ATTN_KV_W
OFF_SV = OFF_SU + SGU_W
OFF_CQ = OFF_SV + SGU_W
OFF_CK = OFF_CQ + GLA_W
OFF_CV = OFF_CK + GLA_W
OFF_CG = OFF_CV + GLA_W
OFF_LR = OFF_CG + GLA_W
IN_MAIN_W = OFF_LR
MIX_W = ATTN_Q_W + SGU_W + GLA_W

FFN_TILE_M = 512
MXU_WIDTH = 256
FFN_CHUNK_BOUNDS = ((0, 6 * MXU_WIDTH), (6 * MXU_WIDTH, D_FF))
MIX_TILE_T = 512
VMEM_LIMIT_BYTES = 56 * 1024 * 1024


def _dot(a, b):
    return jnp.dot(a, b, preferred_element_type=F32)


def _dot_nt(a, b):
    return lax.dot_general(a, b, (((1,), (1,)), ((), ())), preferred_element_type=F32)


def _sigmoid(x):
    return 1.0 / (1.0 + jnp.exp(-x))


def _gelu_tanh(x):
    c = 0.7978845608028654
    return x * (0.5 * (1.0 + jnp.tanh(c * (x + 0.044715 * (x * x * x)))))


def _rms_rows(x, gain):
    ms = jnp.mean(x * x, axis=-1, keepdims=True)
    return x * lax.rsqrt(ms + RMS_EPS) * gain


def _group_ids(shape, dim, group):
    return lax.broadcasted_iota(jnp.int32, shape, dim) // group


def _group_rms(x, gain, same_group):
    ss = _dot((x * x).astype(BF16), same_group)
    return x * lax.rsqrt(ss * (1.0 / HEAD_DIM) + RMS_EPS) * gain


FFN_W_STEPS = D_FF // MXU_WIDTH


def _ffn_kernel(x_ref, gain_ref, wg_ref, wu_ref, wd_ref, o_ref, wg_s, wu_s, wd_s):
    i = pl.program_id(0)

    @pl.when(i < FFN_W_STEPS)
    def _():
        for c in range(FFN_W_STEPS):
            @pl.when(i == c)
            def _(c=c):
                slab = slice(c * MXU_WIDTH, (c + 1) * MXU_WIDTH)
                wg_s[:, slab] = wg_ref[...].astype(BF16)
                wu_s[:, slab] = wu_ref[...].astype(BF16)
                wd_s[slab, :] = wd_ref[...].astype(BF16)

    @pl.when(i >= FFN_W_STEPS)
    def _():
        x = x_ref[...]
        h = _rms_rows(x, gain_ref[...]).astype(BF16)
        y = None
        for lo, hi in FFN_CHUNK_BOUNDS:
            gate = _dot(h, wg_s[:, lo:hi])
            up = _dot(h, wu_s[:, lo:hi])
            act = (gate * _sigmoid(gate) * up).astype(BF16)
            part = _dot(act, wd_s[lo:hi, :])
            y = part if y is None else y + part
        o_ref[...] = x + 0.5 * y


def _resident(shape):
    nd = len(shape)
    return pl.BlockSpec(shape, lambda *_: (0,) * nd, pipeline_mode=pl.Buffered(1))


def _ffn(x2d, gain, wg, wu, wd, layer):
    m = x2d.shape[0]
    tiles = m // FFN_TILE_M
    tile = lambda i: jnp.maximum(i - FFN_W_STEPS, 0)
    slab = lambda i: jnp.minimum(i, FFN_W_STEPS - 1)
    return pl.pallas_call(
        _ffn_kernel,
        out_shape=jax.ShapeDtypeStruct((m, D_MODEL), F32),
        grid=(FFN_W_STEPS + tiles,),
        in_specs=[
            pl.BlockSpec((FFN_TILE_M, D_MODEL), lambda i: (tile(i), 0)),
            _resident((1, D_MODEL)),
            pl.BlockSpec((None, D_MODEL, MXU_WIDTH), lambda i: (layer, 0, slab(i))),
            pl.BlockSpec((None, D_MODEL, MXU_WIDTH), lambda i: (layer, 0, slab(i))),
            pl.BlockSpec((None, MXU_WIDTH, D_MODEL), lambda i: (layer, slab(i), 0)),
        ],
        out_specs=pl.BlockSpec((FFN_TILE_M, D_MODEL), lambda i: (tile(i), 0)),
        scratch_shapes=[
            pltpu.VMEM((D_MODEL, D_FF), BF16),
            pltpu.VMEM((D_MODEL, D_FF), BF16),
            pltpu.VMEM((D_FF, D_MODEL), BF16),
        ],
        compiler_params=pltpu.CompilerParams(
            dimension_semantics=("arbitrary",), vmem_limit_bytes=VMEM_LIMIT_BYTES),
        name="swiglu_ffn",
    )(x2d, gain, wg, wu, wd)


MIX_W_STEPS = IN_MAIN_W // MXU_WIDTH
MIX_TILES_PER_SEQ = SEQ // MIX_TILE_T


def _mixer_kernel(x_ref, gain_ref, w_in_f32_ref, w_lr_ref, qg_ref, kg_ref, sink_ref, svg_ref,
                  sgu_w_ref, sgu_b_ref, gup_ref, gb_ref, og_ref, w_out_f32_ref,
                  o_ref, w_in_s, w_out_s, *scratch):
    i = pl.program_id(0)

    @pl.when(i < MIX_W_STEPS)
    def _():
        for c in range(MIX_W_STEPS):
            @pl.when(i == c)
            def _(c=c):
                slab = slice(c * MXU_WIDTH, (c + 1) * MXU_WIDTH)
                w_in_s[:, slab] = w_in_f32_ref[...].astype(BF16)
                if slab.stop <= MIX_W:
                    w_out_s[slab, :] = w_out_f32_ref[...].astype(BF16)

    @pl.when(i >= MIX_W_STEPS)
    def _():
        _mixer_token_step((i - MIX_W_STEPS) % MIX_TILES_PER_SEQ, x_ref, gain_ref, w_in_s, w_lr_ref,
                          qg_ref, kg_ref, sink_ref, svg_ref, sgu_w_ref, sgu_b_ref, gup_ref, gb_ref,
                          og_ref, w_out_s, o_ref, *scratch)


def _mixer_token_step(ti, x_ref, gain_ref, w_in_ref, w_lr_ref, qg_ref, kg_ref, sink_ref, svg_ref,
                      sgu_w_ref, sgu_b_ref, gup_ref, gb_ref, og_ref, w_out_ref,
                      o_ref, kbuf, vbuf, state, band_bias):
    tt = MIX_TILE_T
    rows = ATTN_GROUP * WINDOW
    kj = lax.broadcasted_iota(jnp.int32, (rows, 2 * WINDOW), 1)

    @pl.when(ti > 0)
    def _():
        kbuf[0:WINDOW, :] = kbuf[tt:tt + WINDOW, :]
        vbuf[0:WINDOW, :] = vbuf[tt:tt + WINDOW, :]

    @pl.when(ti == 0)
    def _():
        kbuf[0:WINDOW, :] = jnp.zeros((WINDOW, 2 * GLA_W), BF16)
        vbuf[0:WINDOW, :] = jnp.zeros((WINDOW, 2 * GLA_W), BF16)
        state[...] = jnp.zeros_like(state)
        rel = lax.broadcasted_iota(jnp.int32, (rows, 2 * WINDOW), 0) % WINDOW + WINDOW - kj
        band_bias[...] = jnp.where((rel >= 0) & (rel < WINDOW), 0.0, NEG_INF)

    x = x_ref[...]
    h = _rms_rows(x, gain_ref[...]).astype(BF16)
    p_attn = _dot(h, w_in_ref[:, OFF_AQ:OFF_SU])
    lr = _dot(h, w_lr_ref[...])
    env = {}

    def project(name, lo):
        env[name] = _dot(h, w_in_ref[:, lo:lo + MXU_WIDTH])

    same256 = (_group_ids((GLA_W, GLA_W), 0, HEAD_DIM)
               == _group_ids((GLA_W, GLA_W), 1, HEAD_DIM))
    ones256 = jnp.where(same256, 1.0, 0.0).astype(BF16)
    lane_grp = _group_ids((1, GLA_W), 1, HEAD_DIM)
    scale = HEAD_DIM ** -0.5

    C = GLA_CHUNK
    chunks = [slice(n * C, (n + 1) * C) for n in range(tt // C)]
    logits = _dot(lr.astype(BF16), gup_ref[...]) + gb_ref[...]
    g_log = -(jnp.maximum(-logits, 0.0) + jnp.log(1.0 + jnp.exp(-jnp.abs(logits)))) * (1.0 / GLA_TAU)
    g_hi = g_log.astype(BF16)
    g_lo = (g_log - g_hi.astype(F32)).astype(BF16)

    q_slabs = []
    for c in range(ATTN_KV_HEADS):
        qs = p_attn[:, OFF_AQ + c * GLA_W: OFF_AQ + (c + 1) * GLA_W]
        q_slabs.append(_group_rms(qs, qg_ref[...], ones256) * (scale * LOG2_E))
    kn = _group_rms(p_attn[:, OFF_AK:OFF_AK + ATTN_KV_W], kg_ref[...],
                    ones256[:ATTN_KV_W, :ATTN_KV_W])

    project("su", OFF_SU)
    project("sv", OFF_SV)

    tri = jnp.where(lax.broadcasted_iota(jnp.int32, (C, C), 1)
                    <= lax.broadcasted_iota(jnp.int32, (C, C), 0), 1.0, 0.0).astype(BF16)
    b = jnp.concatenate([_dot(tri, g_hi[r, :]) + _dot(tri, g_lo[r, :]) for r in chunks], axis=0)
    b_last = [b[r.stop - 1:r.stop, :] for r in chunks]
    b_end = jnp.concatenate([jnp.broadcast_to(bl, (C, GLA_W)) for bl in b_last], axis=0)
    decay_q = jnp.exp(b)
    decay_k_intra = jnp.exp(-b)
    decay_k_state = jnp.exp(b_end - b)
    decay_state = [jnp.exp(bl) for bl in b_last]

    low_half = lax.broadcasted_iota(jnp.int32, (1, ATTN_KV_W), 1) < HEAD_DIM

    def replicate(a):
        swapped = pltpu.roll(a, HEAD_DIM, axis=1)
        h0 = jnp.where(low_half, a, swapped).astype(BF16)
        h1 = jnp.where(low_half, swapped, a).astype(BF16)
        return jnp.concatenate([h0, h0, h1, h1], axis=1)

    kbuf[WINDOW:WINDOW + tt, :] = replicate(kn)
    vbuf[WINDOW:WINDOW + tt, :] = replicate(p_attn[:, OFF_AV:OFF_AV + ATTN_KV_W])

    row_head = _group_ids((rows, 1), 0, WINDOW)
    sink_cols = []
    for c in range(ATTN_KV_HEADS):
        sink_col = jnp.zeros((rows, 1), F32)
        for g in range(ATTN_GROUP):
            sink_col = jnp.where(row_head == g, sink_ref[ATTN_GROUP * c + g] * LOG2_E, sink_col)
        sink_cols.append(sink_col)
    first_key = jnp.where(ti > 0, 0, WINDOW)

    def scores(c, j):
        qblk = q_slabs[c][j * WINDOW:(j + 1) * WINDOW, :]
        qstack = jnp.concatenate(
            [jnp.where(lane_grp == g, qblk, 0.0) for g in range(ATTN_GROUP)],
            axis=0).astype(BF16)
        k2 = kbuf[j * WINDOW:(j + 2) * WINDOW, c * GLA_W:(c + 1) * GLA_W]
        s = _dot_nt(qstack, k2) + band_bias[...]
        if j == 0:
            s = jnp.where(kj >= first_key, s, NEG_INF)
        return s

    def attend(c, j, s):
        m = jnp.maximum(jnp.max(s, axis=-1, keepdims=True), sink_cols[c])
        e = jnp.exp2(s - m)
        den = jnp.sum(e, axis=-1, keepdims=True) + jnp.exp2(sink_cols[c] - m)
        v2 = vbuf[j * WINDOW:(j + 2) * WINDOW, c * GLA_W:(c + 1) * GLA_W]
        o = _dot(e.astype(BF16), v2) * (1.0 / den)
        out = jnp.zeros((WINDOW, GLA_W), F32)
        for g in range(ATTN_GROUP):
            out = jnp.where(lane_grp == g, o[g * WINDOW:(g + 1) * WINDOW, :], out)
        return out.astype(BF16)

    def gating_u():
        env["u"] = _gelu_tanh(env["su"])

    def gating_v():
        vn = _group_rms(_gelu_tanh(env["sv"]), svg_ref[...], ones256)
        wt = lax.broadcasted_iota(jnp.int32, (SGU_CHUNK, SGU_GROUPS * SGU_CHUNK), 0)
        ws = lax.broadcasted_iota(jnp.int32, (SGU_CHUNK, SGU_GROUPS * SGU_CHUNK), 1) % SGU_CHUNK
        w_causal = jnp.where(ws <= wt, sgu_w_ref[...], 0.0).astype(BF16)
        out_b = []
        for j in range(tt // SGU_CHUNK):
            vchunk = vn[j * SGU_CHUNK:(j + 1) * SGU_CHUNK, :]
            vblk = jnp.concatenate(
                [jnp.where(lane_grp == g, vchunk, 0.0) for g in range(SGU_GROUPS)],
                axis=0).astype(BF16)
            sg = _dot(w_causal, vblk) + sgu_b_ref[...]
            out_b.append((env["u"][j * SGU_CHUNK:(j + 1) * SGU_CHUNK, :] * sg).astype(BF16))
        env["out_b"] = jnp.concatenate(out_b, axis=0)

    def gla_q():
        qd_f = (env["cq"] * scale) * decay_q
        env["qd"] = qd_f.astype(BF16)
        env["qexp"] = [
            jnp.concatenate([jnp.where(lane_grp == hh, qd_f[r, :], 0.0) for hh in range(GLA_HEADS)],
                            axis=0).astype(BF16) for r in chunks]

    def gla_k():
        ki = (env["ck"] * decay_k_intra).astype(BF16)
        env["ks"] = (env["ck"] * decay_k_state).astype(BF16)
        env["attn"] = [_dot_nt(env["qexp"][n], ki[r, :]) for n, r in enumerate(chunks)]

    def gla_v():
        cv = env["cv"]
        env["vc"] = cv.astype(BF16)
        env["delta"] = [
            jnp.where(same256, _dot(cv[r, :].T.astype(BF16), env["ks"][r, :]), 0.0)
            for r in chunks]

    def gla_intra():
        hrows = GLA_HEADS * C
        causal = (lax.broadcasted_iota(jnp.int32, (hrows, C), 1)
                  <= lax.broadcasted_iota(jnp.int32, (hrows, C), 0) % C)
        o_intra = []
        for n, r in enumerate(chunks):
            rfull = _dot(jnp.where(causal, env["attn"][n], 0.0).astype(BF16), env["vc"][r, :])
            oi = jnp.zeros((C, GLA_W), F32)
            for hh in range(GLA_HEADS):
                oi = jnp.where(lane_grp == hh, rfull[hh * C:(hh + 1) * C, :], oi)
            o_intra.append(oi)
        env["o_intra"] = o_intra

    def gla_recurrence():
        st = state[...]
        o_chunks = []
        for n, r in enumerate(chunks):
            o_chunks.append(env["o_intra"][n] + _dot_nt(env["qd"][r, :], st.astype(BF16)))
            st = st * decay_state[n] + env["delta"][n]
        state[...] = st
        env["oc"] = jnp.concatenate(o_chunks, axis=0)

    order = [(c, j) for c in range(ATTN_KV_HEADS) for j in range(tt // WINDOW)]
    blocks = [[] for _ in range(ATTN_KV_HEADS)]

    def out_proj(name, piece, lo):
        env[name] = _dot(piece, w_out_ref[lo:lo + MXU_WIDTH, :])

    between_blocks = [
        [lambda: project("cq", OFF_CQ), gating_u],
        [lambda: project("ck", OFF_CK), gating_v],
        [lambda: project("cv", OFF_CV), gla_q],
        [lambda: project("cg", OFF_CG), gla_k],
        [gla_v, lambda: out_proj("y_b", env["out_b"], ATTN_Q_W)],
        [gla_intra, lambda: out_proj("y_a0", jnp.concatenate(blocks[0], axis=0), 0)],
        [gla_recurrence],
        [],
    ]
    s_next = scores(*order[0])
    for i, (c, j) in enumerate(order):
        s_cur = s_next
        if i + 1 < len(order):
            s_next = scores(*order[i + 1])
        for work in between_blocks[i]:
            work()
        blocks[c].append(attend(c, j, s_cur))
    out_proj("y_a1", jnp.concatenate(blocks[1], axis=0), GLA_W)
    cgate = env["cg"]
    oc = _group_rms(env["oc"], og_ref[...], ones256) * (cgate * _sigmoid(cgate))
    out_proj("y_c", oc.astype(BF16), ATTN_Q_W + SGU_W)
    o_ref[...] = x + env["y_a0"] + env["y_b"] + env["y_a1"] + env["y_c"]


def _mixer(x, gain, w_in, w_lr, qg, kg, sinks, svg, sgu_w, sgu_b, gup, gb, og, w_out, layer):
    B, T, _ = x.shape
    tt = MIX_TILE_T
    assert T // tt == MIX_TILES_PER_SEQ
    tile = lambda i: jnp.maximum(i - MIX_W_STEPS, 0)
    x_block = pl.BlockSpec((None, tt, D_MODEL),
                           lambda i: (tile(i) // MIX_TILES_PER_SEQ, tile(i) % MIX_TILES_PER_SEQ, 0))
    vmem = [
        _resident((1, D_MODEL)),
        pl.BlockSpec((None, D_MODEL, MXU_WIDTH),
                     lambda i: (layer, 0, jnp.minimum(i, MIX_W_STEPS - 1))),
        _resident((D_MODEL, GLA_GATE_RANK)),
        _resident((1, GLA_W)),
        _resident((1, ATTN_KV_W)),
        pl.BlockSpec(memory_space=pltpu.SMEM),
        _resident((1, SGU_W)),
        _resident((SGU_CHUNK, SGU_GROUPS * SGU_CHUNK)),
        _resident((SGU_CHUNK, SGU_W)),
        _resident((GLA_GATE_RANK, GLA_W)),
        _resident((1, GLA_W)),
        _resident((1, GLA_W)),
        pl.BlockSpec((None, MXU_WIDTH, D_MODEL),
                     lambda i: (layer, jnp.minimum(i, MIX_W // MXU_WIDTH - 1), 0)),
    ]
    return pl.pallas_call(
        _mixer_kernel,
        out_shape=jax.ShapeDtypeStruct((B, T, D_MODEL), F32),
        grid=(MIX_W_STEPS + B * MIX_TILES_PER_SEQ,),
        in_specs=[x_block] + vmem,
        out_specs=x_block,
        scratch_shapes=[
            pltpu.VMEM((D_MODEL, IN_MAIN_W), BF16),
            pltpu.VMEM((MIX_W, D_MODEL), BF16),
            pltpu.VMEM((WINDOW + tt, 2 * GLA_W), BF16),
            pltpu.VMEM((WINDOW + tt, 2 * GLA_W), BF16),
            pltpu.VMEM((GLA_W, GLA_W), F32),
            pltpu.VMEM((ATTN_GROUP * WINDOW, 2 * WINDOW), F32),
        ],
        compiler_params=pltpu.CompilerParams(
            dimension_semantics=("arbitrary",), vmem_limit_bytes=VMEM_LIMIT_BYTES),
        name="token_mixer",
    )(x, gain, w_in, w_lr, qg, kg, sinks, svg, sgu_w, sgu_b, gup, gb, og, w_out)


def kernel(x, ffn1_norm, ffn1_w_gate, ffn1_w_up, ffn1_w_down, mix_norm, w_in, attn_q_norm,
           attn_k_norm, attn_sinks, sgu_v_norm, sgu_w, sgu_b, gla_w_gate_up, gla_b_gate,
           gla_out_norm, w_out, ffn2_norm, ffn2_w_gate, ffn2_w_up, ffn2_w_down):
    B, T, D = x.shape
    row = lambda v: v.reshape(1, -1).astype(F32)
    for l in range(DEPTH):
        x = _ffn(x.reshape(B * T, D), row(ffn1_norm[l]), ffn1_w_gate, ffn1_w_up,
                 ffn1_w_down, l).reshape(B, T, D)
        x = _mixer(
            x, row(mix_norm[l]),
            w_in, w_in[l, :, IN_MAIN_W:].astype(BF16),
            row(jnp.tile(attn_q_norm[l], ATTN_GROUP)), row(jnp.tile(attn_k_norm[l], ATTN_KV_HEADS)),
            attn_sinks[l].astype(F32), row(sgu_v_norm[l]),
            jnp.transpose(sgu_w[l], (1, 0, 2)).reshape(SGU_CHUNK, SGU_GROUPS * SGU_CHUNK).astype(F32),
            jnp.repeat(sgu_b[l].T, HEAD_DIM, axis=1).astype(F32),
            gla_w_gate_up[l].astype(BF16), row(gla_b_gate[l]),
            row(jnp.tile(gla_out_norm[l], GLA_HEADS)), w_out, l)
        x = _ffn(x.reshape(B * T, D), row(ffn2_norm[l]), ffn2_w_gate, ffn2_w_up,
                 ffn2_w_down, l).reshape(B, T, D)
    return x
```

```python
import jax
import jax.numpy as jnp
from jax import lax
from jax.experimental import pallas as pl
from jax.experimental.pallas import tpu as pltpu

F32 = jnp.float32
BF16 = jnp.bfloat16

D_MODEL = 1024
D_FF = 2816
SEQ = 4096
DEPTH = 2
HEAD_DIM = 64
ATTN_Q_HEADS = 8
ATTN_KV_HEADS = 2
ATTN_GROUP = ATTN_Q_HEADS // ATTN_KV_HEADS
WINDOW = 128
SGU_GROUPS = 4
SGU_CHUNK = 128
GLA_HEADS = 4
GLA_GATE_RANK = 16
GLA_TAU = 16.0
GLA_CHUNK = 64
RMS_EPS = 1e-6
NEG_INF = -1e30
LOG2_E = 1.4426950408889634

ATTN_Q_W = ATTN_Q_HEADS * HEAD_DIM
ATTN_KV_W = ATTN_KV_HEADS * HEAD_DIM
SGU_W = SGU_GROUPS * HEAD_DIM
GLA_W = GLA_HEADS * HEAD_DIM
OFF_AQ = 0
OFF_AK = OFF_AQ + ATTN_Q_W
OFF_AV = OFF_AK + ATTN_KV_W
OFF_SU = OFF_AV + ATTN_KV_W
OFF_SV = OFF_SU + SGU_W
OFF_CQ = OFF_SV + SGU_W
OFF_CK = OFF_CQ + GLA_W
OFF_CV = OFF_CK + GLA_W
OFF_CG = OFF_CV + GLA_W
OFF_LR = OFF_CG + GLA_W
IN_MAIN_W = OFF_LR
MIX_W = ATTN_Q_W + SGU_W + GLA_W

FFN_TILE_M = 1024
MXU_WIDTH = 256
FFN_CHUNK_BOUNDS = ((0, 4 * MXU_WIDTH), (4 * MXU_WIDTH, 8 * MXU_WIDTH), (8 * MXU_WIDTH, D_FF))
MIX_TILE_T = 512
VMEM_LIMIT_BYTES = 56 * 1024 * 1024


def _dot(a, b):
    return jnp.dot(a, b, preferred_element_type=F32)


def _dot_nt(a, b):
    return lax.dot_general(a, b, (((1,), (1,)), ((), ())), preferred_element_type=F32)


def _sigmoid(x):
    return 1.0 / (1.0 + jnp.exp(-x))


def _gelu_tanh(x):
    c = 0.7978845608028654
    return x * (0.5 * (1.0 + jnp.tanh(c * (x + 0.044715 * (x * x * x)))))


def _rms_rows(x, gain):
    ms = jnp.mean(x * x, axis=-1, keepdims=True)
    return x * lax.rsqrt(ms + RMS_EPS) * gain


def _group_ids(shape, dim, group):
    return lax.broadcasted_iota(jnp.int32, shape, dim) // group


def _group_rms(x, gain, same_group):
    ss = _dot((x * x).astype(BF16), same_group)
    return x * lax.rsqrt(ss * (1.0 / HEAD_DIM) + RMS_EPS) * gain


FFN_W_STEPS = D_FF // MXU_WIDTH


def _ffn_kernel(x_ref, gain_ref, wg_ref, wu_ref, wd_ref, o_ref, wg_s, wu_s, wd_s):
    i = pl.program_id(0)

    @pl.when(i < FFN_W_STEPS)
    def _():
        for c in range(FFN_W_STEPS):
            @pl.when(i == c)
            def _(c=c):
                slab = slice(c * MXU_WIDTH, (c + 1) * MXU_WIDTH)
                wg_s[:, slab] = wg_ref[...].astype(BF16)
                wu_s[:, slab] = wu_ref[...].astype(BF16)
                wd_s[slab, :] = wd_ref[...].astype(BF16)

    @pl.when(i >= FFN_W_STEPS)
    def _():
        x = x_ref[...]
        h = _rms_rows(x, gain_ref[...]).astype(BF16)
        y = None
        for lo, hi in FFN_CHUNK_BOUNDS:
            gate = _dot(h, wg_s[:, lo:hi])
            up = _dot(h, wu_s[:, lo:hi])
            act = (gate * _sigmoid(gate) * up).astype(BF16)
            part = _dot(act, wd_s[lo:hi, :])
            y = part if y is None else y + part
        o_ref[...] = x + 0.5 * y


def _resident(shape):
    nd = len(shape)
    return pl.BlockSpec(shape, lambda *_: (0,) * nd, pipeline_mode=pl.Buffered(1))


def _ffn(x2d, gain, wg, wu, wd, layer):
    m = x2d.shape[0]
    tiles = m // FFN_TILE_M
    tile = lambda i: jnp.maximum(i - FFN_W_STEPS, 0)
    slab = lambda i: jnp.minimum(i, FFN_W_STEPS - 1)
    return pl.pallas_call(
        _ffn_kernel,
        out_shape=jax.ShapeDtypeStruct((m, D_MODEL), F32),
        grid=(FFN_W_STEPS + tiles,),
        in_specs=[
            pl.BlockSpec((FFN_TILE_M, D_MODEL), lambda i: (tile(i), 0)),
            _resident((1, D_MODEL)),
            pl.BlockSpec((None, D_MODEL, MXU_WIDTH), lambda i: (layer, 0, slab(i))),
            pl.BlockSpec((None, D_MODEL, MXU_WIDTH), lambda i: (layer, 0, slab(i))),
            pl.BlockSpec((None, MXU_WIDTH, D_MODEL), lambda i: (layer, slab(i), 0)),
        ],
        out_specs=pl.BlockSpec((FFN_TILE_M, D_MODEL), lambda i: (tile(i), 0)),
        scratch_shapes=[
            pltpu.VMEM((D_MODEL, D_FF), BF16),
            pltpu.VMEM((D_MODEL, D_FF), BF16),
            pltpu.VMEM((D_FF, D_MODEL), BF16),
        ],
        compiler_params=pltpu.CompilerParams(
            dimension_semantics=("arbitrary",), vmem_limit_bytes=VMEM_LIMIT_BYTES),
        name="swiglu_ffn",
    )(x2d, gain, wg, wu, wd)


MIX_W_STEPS = IN_MAIN_W // MXU_WIDTH
MIX_TILES_PER_SEQ = SEQ // MIX_TILE_T


def _mixer_kernel(x_ref, gain_ref, w_in_f32_ref, w_lr_ref, qg_ref, kg_ref, sink_ref, svg_ref,
                  sgu_w_ref, sgu_b_ref, gup_ref, gb_ref, og_ref, w_out_f32_ref,
                  o_ref, w_in_s, w_out_s, *scratch):
    i = pl.program_id(0)

    @pl.when(i < MIX_W_STEPS)
    def _():
        for c in range(MIX_W_STEPS):
            @pl.when(i == c)
            def _(c=c):
                slab = slice(c * MXU_WIDTH, (c + 1) * MXU_WIDTH)
                w_in_s[:, slab] = w_in_f32_ref[...].astype(BF16)
                if slab.stop <= MIX_W:
                    w_out_s[slab, :] = w_out_f32_ref[...].astype(BF16)

    @pl.when(i >= MIX_W_STEPS)
    def _():
        _mixer_token_step((i - MIX_W_STEPS) % MIX_TILES_PER_SEQ, x_ref, gain_ref, w_in_s, w_lr_ref,
                          qg_ref, kg_ref, sink_ref, svg_ref, sgu_w_ref, sgu_b_ref, gup_ref, gb_ref,
                          og_ref, w_out_s, o_ref, *scratch)


def _mixer_token_step(ti, x_ref, gain_ref, w_in_ref, w_lr_ref, qg_ref, kg_ref, sink_ref, svg_ref,
                      sgu_w_ref, sgu_b_ref, gup_ref, gb_ref, og_ref, w_out_ref,
                      o_ref, kbuf, vbuf, state, band_bias):
    tt = MIX_TILE_T
    rows = ATTN_GROUP * WINDOW
    kj = lax.broadcasted_iota(jnp.int32, (rows, 2 * WINDOW), 1)

    @pl.when(ti > 0)
    def _():
        kbuf[0:WINDOW, :] = kbuf[tt:tt + WINDOW, :]
        vbuf[0:WINDOW, :] = vbuf[tt:tt + WINDOW, :]

    @pl.when(ti == 0)
    def _():
        kbuf[0:WINDOW, :] = jnp.zeros((WINDOW, 2 * GLA_W), BF16)
        vbuf[0:WINDOW, :] = jnp.zeros((WINDOW, 2 * GLA_W), BF16)
        state[...] = jnp.zeros_like(state)
        rel = lax.broadcasted_iota(jnp.int32, (rows, 2 * WINDOW), 0) % WINDOW + WINDOW - kj
        band_bias[...] = jnp.where((rel >= 0) & (rel < WINDOW), 0.0, NEG_INF)

    x = x_ref[...]
    h = _rms_rows(x, gain_ref[...]).astype(BF16)
    p_attn = _dot(h, w_in_ref[:, OFF_AQ:OFF_SU])
    lr = _dot(h, w_lr_ref[...])
    env = {}

    def project(name, lo):
        env[name] = _dot(h, w_in_ref[:, lo:lo + MXU_WIDTH])

    same256 = (_group_ids((GLA_W, GLA_W), 0, HEAD_DIM)
               == _group_ids((GLA_W, GLA_W), 1, HEAD_DIM))
    ones256 = jnp.where(same256, 1.0, 0.0).astype(BF16)
    lane_grp = _group_ids((1, GLA_W), 1, HEAD_DIM)
    scale = HEAD_DIM ** -0.5

    C = GLA_CHUNK
    chunks = [slice(n * C, (n + 1) * C) for n in range(tt // C)]
    logits = _dot(lr.astype(BF16), gup_ref[...]) + gb_ref[...]
    g_log = -(jnp.maximum(-logits, 0.0) + jnp.log(1.0 + jnp.exp(-jnp.abs(logits)))) * (1.0 / GLA_TAU)
    g_hi = g_log.astype(BF16)
    g_lo = (g_log - g_hi.astype(F32)).astype(BF16)

    q_slabs = []
    for c in range(ATTN_KV_HEADS):
        qs = p_attn[:, OFF_AQ + c * GLA_W: OFF_AQ + (c + 1) * GLA_W]
        q_slabs.append(_group_rms(qs, qg_ref[...], ones256) * (scale * LOG2_E))
    kn = _group_rms(p_attn[:, OFF_AK:OFF_AK + ATTN_KV_W], kg_ref[...],
                    ones256[:ATTN_KV_W, :ATTN_KV_W])

    project("su", OFF_SU)
    project("sv", OFF_SV)

    tri = jnp.where(lax.broadcasted_iota(jnp.int32, (C, C), 1)
                    <= lax.broadcasted_iota(jnp.int32, (C, C), 0), 1.0, 0.0).astype(BF16)
    b = jnp.concatenate([_dot(tri, g_hi[r, :]) + _dot(tri, g_lo[r, :]) for r in chunks], axis=0)
    b_last = [b[r.stop - 1:r.stop, :] for r in chunks]
    b_end = jnp.concatenate([jnp.broadcast_to(bl, (C, GLA_W)) for bl in b_last], axis=0)
    decay_q = jnp.exp(b)
    decay_k_intra = jnp.exp(-b)
    decay_k_state = jnp.exp(b_end - b)
    decay_state = [jnp.exp(bl) for bl in b_last]

    low_half = lax.broadcasted_iota(jnp.int32, (1, ATTN_KV_W), 1) < HEAD_DIM

    def replicate(a):
        swapped = pltpu.roll(a, HEAD_DIM, axis=1)
        h0 = jnp.where(low_half, a, swapped).astype(BF16)
        h1 = jnp.where(low_half, swapped, a).astype(BF16)
        return jnp.concatenate([h0, h0, h1, h1], axis=1)

    kbuf[WINDOW:WINDOW + tt, :] = replicate(kn)
    vbuf[WINDOW:WINDOW + tt, :] = replicate(p_attn[:, OFF_AV:OFF_AV + ATTN_KV_W])

    row_head = _group_ids((rows, 1), 0, WINDOW)
    sink_cols = []
    for c in range(ATTN_KV_HEADS):
        sink_col = jnp.zeros((rows, 1), F32)
        for g in range(ATTN_GROUP):
            sink_col = jnp.where(row_head == g, sink_ref[ATTN_GROUP * c + g] * LOG2_E, sink_col)
        sink_cols.append(sink_col)
    first_key = jnp.where(ti > 0, 0, WINDOW)

    def scores(c, j):
        qblk = q_slabs[c][j * WINDOW:(j + 1) * WINDOW, :]
        qstack = jnp.concatenate(
            [jnp.where(lane_grp == g, qblk, 0.0) for g in range(ATTN_GROUP)],
            axis=0).astype(BF16)
        k2 = kbuf[j * WINDOW:(j + 2) * WINDOW, c * GLA_W:(c + 1) * GLA_W]
        s = _dot_nt(qstack, k2) + band_bias[...]
        if j == 0:
            s = jnp.where(kj >= first_key, s, NEG_INF)
        return s

    def attend(c, j, s):
        m = jnp.maximum(jnp.max(s, axis=-1, keepdims=True), sink_cols[c])
        e = jnp.exp2(s - m)
        den = jnp.sum(e, axis=-1, keepdims=True) + jnp.exp2(sink_cols[c] - m)
        v2 = vbuf[j * WINDOW:(j + 2) * WINDOW, c * GLA_W:(c + 1) * GLA_W]
        o = _dot(e.astype(BF16), v2) * (1.0 / den)
        out = jnp.zeros((WINDOW, GLA_W), F32)
        for g in range(ATTN_GROUP):
            out = jnp.where(lane_grp == g, o[g * WINDOW:(g + 1) * WINDOW, :], out)
        return out.astype(BF16)

    def gating_u():
        env["u"] = _gelu_tanh(env["su"])

    def gating_v():
        vn = _group_rms(_gelu_tanh(env["sv"]), svg_ref[...], ones256)
        wt = lax.broadcasted_iota(jnp.int32, (SGU_CHUNK, SGU_GROUPS * SGU_CHUNK), 0)
        ws = lax.broadcasted_iota(jnp.int32, (SGU_CHUNK, SGU_GROUPS * SGU_CHUNK), 1) % SGU_CHUNK
        w_causal = jnp.where(ws <= wt, sgu_w_ref[...], 0.0).astype(BF16)
        out_b = []
        for j in range(tt // SGU_CHUNK):
            vchunk = vn[j * SGU_CHUNK:(j + 1) * SGU_CHUNK, :]
            vblk = jnp.concatenate(
                [jnp.where(lane_grp == g, vchunk, 0.0) for g in range(SGU_GROUPS)],
                axis=0).astype(BF16)
            sg = _dot(w_causal, vblk) + sgu_b_ref[...]
            out_b.append((env["u"][j * SGU_CHUNK:(j + 1) * SGU_CHUNK, :] * sg).astype(BF16))
        env["out_b"] = jnp.concatenate(out_b, axis=0)

    def gla_q():
        qd_f = (env["cq"] * scale) * decay_q
        env["qd"] = qd_f.astype(BF16)
        env["qexp"] = [
            jnp.concatenate([jnp.where(lane_grp == hh, qd_f[r, :], 0.0) for hh in range(GLA_HEADS)],
                            axis=0).astype(BF16) for r in chunks]

    def gla_k():
        ki = (env["ck"] * decay_k_intra).astype(BF16)
        env["ks"] = (env["ck"] * decay_k_state).astype(BF16)
        env["attn"] = [_dot_nt(env["qexp"][n], ki[r, :]) for n, r in enumerate(chunks)]

    def gla_v():
        cv = env["cv"]
        env["vc"] = cv.astype(BF16)
        env["delta"] = [
            jnp.where(same256, _dot(cv[r, :].T.astype(BF16), env["ks"][r, :]), 0.0)
            for r in chunks]

    def gla_intra():
        hrows = GLA_HEADS * C
        causal = (lax.broadcasted_iota(jnp.int32, (hrows, C), 1)
                  <= lax.broadcasted_iota(jnp.int32, (hrows, C), 0) % C)
        o_intra = []
        for n, r in enumerate(chunks):
            rfull = _dot(jnp.where(causal, env["attn"][n], 0.0).astype(BF16), env["vc"][r, :])
            oi = jnp.zeros((C, GLA_W), F32)
            for hh in range(GLA_HEADS):
                oi = jnp.where(lane_grp == hh, rfull[hh * C:(hh + 1) * C, :], oi)
            o_intra.append(oi)
        env["o_intra"] = o_intra

    def gla_recurrence():
        st = state[...]
        o_chunks = []
        for n, r in enumerate(chunks):
            o_chunks.append(env["o_intra"][n] + _dot_nt(env["qd"][r, :], st.astype(BF16)))
            st = st * decay_state[n] + env["delta"][n]
        state[...] = st
        env["oc"] = jnp.concatenate(o_chunks, axis=0)

    order = [(c, j) for c in range(ATTN_KV_HEADS) for j in range(tt // WINDOW)]
    blocks = [[] for _ in range(ATTN_KV_HEADS)]

    def out_proj(name, piece, lo):
        env[name] = _dot(piece, w_out_ref[lo:lo + MXU_WIDTH, :])

    between_blocks = [
        [lambda: project("cq", OFF_CQ), gating_u],
        [lambda: project("ck", OFF_CK), gating_v],
        [lambda: project("cv", OFF_CV), gla_q],
        [lambda: project("cg", OFF_CG), gla_k],
        [gla_v, lambda: out_proj("y_b", env["out_b"], ATTN_Q_W)],
        [gla_intra, lambda: out_proj("y_a0", jnp.concatenate(blocks[0], axis=0), 0)],
        [gla_recurrence],
        [],
    ]
    stride = len(order) // len(between_blocks)
    s_next = scores(*order[0])
    for i, (c, j) in enumerate(order):
        s_cur = s_next
        if i + 1 < len(order):
            s_next = scores(*order[i + 1])
        if i % stride == 0:
            for work in between_blocks[i // stride]:
                work()
        blocks[c].append(attend(c, j, s_cur))
    out_proj("y_a1", jnp.concatenate(blocks[1], axis=0), GLA_W)
    cgate = env["cg"]
    oc = _group_rms(env["oc"], og_ref[...], ones256) * (cgate * _sigmoid(cgate))
    out_proj("y_c", oc.astype(BF16), ATTN_Q_W + SGU_W)
    o_ref[...] = x + env["y_a0"] + env["y_b"] + env["y_a1"] + env["y_c"]


def _mixer(x, gain, w_in, w_lr, qg, kg, sinks, svg, sgu_w, sgu_b, gup, gb, og, w_out, layer):
    B, T, _ = x.shape
    tt = MIX_TILE_T
    assert T // tt == MIX_TILES_PER_SEQ
    tile = lambda i: jnp.maximum(i - MIX_W_STEPS, 0)
    x_block = pl.BlockSpec((None, tt, D_MODEL),
                           lambda i: (tile(i) // MIX_TILES_PER_SEQ, tile(i) % MIX_TILES_PER_SEQ, 0))
    vmem = [
        _resident((1, D_MODEL)),
        pl.BlockSpec((None, D_MODEL, MXU_WIDTH),
                     lambda i: (layer, 0, jnp.minimum(i, MIX_W_STEPS - 1))),
        _resident((D_MODEL, GLA_GATE_RANK)),
        _resident((1, GLA_W)),
        _resident((1, ATTN_KV_W)),
        pl.BlockSpec(memory_space=pltpu.SMEM),
        _resident((1, SGU_W)),
        _resident((SGU_CHUNK, SGU_GROUPS * SGU_CHUNK)),
        _resident((SGU_CHUNK, SGU_W)),
        _resident((GLA_GATE_RANK, GLA_W)),
        _resident((1, GLA_W)),
        _resident((1, GLA_W)),
        pl.BlockSpec((None, MXU_WIDTH, D_MODEL),
                     lambda i: (layer, jnp.minimum(i, MIX_W // MXU_WIDTH - 1), 0)),
    ]
    return pl.pallas_call(
        _mixer_kernel,
        out_shape=jax.ShapeDtypeStruct((B, T, D_MODEL), F32),
        grid=(MIX_W_STEPS + B * MIX_TILES_PER_SEQ,),
        in_specs=[x_block] + vmem,
        out_specs=x_block,
        scratch_shapes=[
            pltpu.VMEM((D_MODEL, IN_MAIN_W), BF16),
            pltpu.VMEM((MIX_W, D_MODEL), BF16),
            pltpu.VMEM((WINDOW + tt, 2 * GLA_W), BF16),
            pltpu.VMEM((WINDOW + tt, 2 * GLA_W), BF16),
            pltpu.VMEM((GLA_W, GLA_W), F32),
            pltpu.VMEM((ATTN_GROUP * WINDOW, 2 * WINDOW), F32),
        ],
        compiler_params=pltpu.CompilerParams(
            dimension_semantics=("arbitrary",), vmem_limit_bytes=VMEM_LIMIT_BYTES),
        name="token_mixer",
    )(x, gain, w_in, w_lr, qg, kg, sinks, svg, sgu_w, sgu_b, gup, gb, og, w_out)


def kernel(x, ffn1_norm, ffn1_w_gate, ffn1_w_up, ffn1_w_down, mix_norm, w_in, attn_q_norm,
           attn_k_norm, attn_sinks, sgu_v_norm, sgu_w, sgu_b, gla_w_gate_up, gla_b_gate,
           gla_out_norm, w_out, ffn2_norm, ffn2_w_gate, ffn2_w_up, ffn2_w_down):
    B, T, D = x.shape
    row = lambda v: v.reshape(1, -1).astype(F32)
    for l in range(DEPTH):
        x = _ffn(x.reshape(B * T, D), row(ffn1_norm[l]), ffn1_w_gate, ffn1_w_up,
                 ffn1_w_down, l).reshape(B, T, D)
        x = _mixer(
            x, row(mix_norm[l]),
            w_in, w_in[l, :, IN_MAIN_W:].astype(BF16),
            row(jnp.tile(attn_q_norm[l], ATTN_GROUP)), row(jnp.tile(attn_k_norm[l], ATTN_KV_HEADS)),
            attn_sinks[l].astype(F32), row(sgu_v_norm[l]),
            jnp.transpose(sgu_w[l], (1, 0, 2)).reshape(SGU_CHUNK, SGU_GROUPS * SGU_CHUNK).astype(F32),
            jnp.repeat(sgu_b[l].T, HEAD_DIM, axis=1).astype(F32),
            gla_w_gate_up[l].astype(BF16), row(gla_b_gate[l]),
            row(jnp.tile(gla_out_norm[l], GLA_HEADS)), w_out, l)
        x = _ffn(x.reshape(B * T, D), row(ffn2_norm[l]), ffn2_w_gate, ffn2_w_up,
                 ffn2_w_down, l).reshape(B, T, D)
    return x
```

```python
import jax
import jax.numpy as jnp
from jax import lax
from jax.experimental import pallas as pl
from jax.experimental.pallas import tpu as pltpu

F32 = jnp.float32
BF16 = jnp.bfloat16

D_MODEL = 1024
D_FF = 2816
SEQ = 4096
DEPTH = 2
HEAD_DIM = 64
ATTN_Q_HEADS = 8
ATTN_KV_HEADS = 2
ATTN_GROUP = ATTN_Q_HEADS // ATTN_KV_HEADS
WINDOW = 128
SGU_GROUPS = 4
SGU_CHUNK = 128
GLA_HEADS = 4
GLA_GATE_RANK = 16
GLA_TAU = 16.0
GLA_CHUNK = 64
RMS_EPS = 1e-6
NEG_INF = -1e30
LOG2_E = 1.4426950408889634

ATTN_Q_W = ATTN_Q_HEADS * HEAD_DIM
ATTN_KV_W = ATTN_KV_HEADS * HEAD_DIM
SGU_W = SGU_GROUPS * HEAD_DIM
GLA_W = GLA_HEADS * HEAD_DIM
OFF_AQ = 0
OFF_AK = OFF_AQ + ATTN_Q_W
OFF_AV = OFF_AK + ATTN_KV_W
OFF_SU = OFF_AV + ATTN_KV_W
OFF_SV = OFF_SU + SGU_W
OFF_CQ = OFF_SV + SGU_W
OFF_CK = OFF_CQ + GLA_W
OFF_CV = OFF_CK + GLA_W
OFF_CG = OFF_CV + GLA_W
OFF_LR = OFF_CG + GLA_W
IN_MAIN_W = OFF_LR
MIX_W = ATTN_Q_W + SGU_W + GLA_W

FFN_TILE_M = 1024
MXU_WIDTH = 256
FFN_CHUNK_BOUNDS = ((0, 4 * MXU_WIDTH), (4 * MXU_WIDTH, 8 * MXU_WIDTH), (8 * MXU_WIDTH, D_FF))
MIX_TILE_T = 512
VMEM_LIMIT_BYTES = 56 * 1024 * 1024


def _dot(a, b):
    return jnp.dot(a, b, preferred_element_type=F32)


def _dot_nt(a, b):
    return lax.dot_general(a, b, (((1,), (1,)), ((), ())), preferred_element_type=F32)


def _sigmoid(x):
    return 1.0 / (1.0 + jnp.exp(-x))


def _gelu_tanh(x):
    c = 0.7978845608028654
    return x * (0.5 * (1.0 + jnp.tanh(c * (x + 0.044715 * (x * x * x)))))


def _rms_rows(x, gain):
    ms = jnp.mean(x * x, axis=-1, keepdims=True)
    return x * lax.rsqrt(ms + RMS_EPS) * gain


def _group_ids(shape, dim, group):
    return lax.broadcasted_iota(jnp.int32, shape, dim) // group


def _group_rms(x, gain, same_group):
    ss = _dot((x * x).astype(BF16), same_group)
    return x * lax.rsqrt(ss * (1.0 / HEAD_DIM) + RMS_EPS) * gain


FFN_W_STEPS = D_FF // MXU_WIDTH


def _ffn_kernel(x_ref, gain_ref, wg_ref, wu_ref, wd_ref, o_ref, wg_s, wu_s, wd_s):
    i = pl.program_id(0)

    @pl.when(i < FFN_W_STEPS)
    def _():
        for c in range(FFN_W_STEPS):
            @pl.when(i == c)
            def _(c=c):
                slab = slice(c * MXU_WIDTH, (c + 1) * MXU_WIDTH)
                wg_s[:, slab] = wg_ref[...].astype(BF16)
                wu_s[:, slab] = wu_ref[...].astype(BF16)
                wd_s[slab, :] = wd_ref[...].astype(BF16)

    @pl.when(i >= FFN_W_STEPS)
    def _():
        x = x_ref[...]
        h = _rms_rows(x, gain_ref[...]).astype(BF16)
        y = None
        for lo, hi in FFN_CHUNK_BOUNDS:
            gate = _dot(h, wg_s[:, lo:hi])
            up = _dot(h, wu_s[:, lo:hi])
            act = (gate * _sigmoid(gate) * up).astype(BF16)
            part = _dot(act, wd_s[lo:hi, :])
            y = part if y is None else y + part
        o_ref[...] = x + 0.5 * y


def _resident(shape):
    nd = len(shape)
    return pl.BlockSpec(shape, lambda *_: (0,) * nd, pipeline_mode=pl.Buffered(1))


def _ffn(x2d, gain, wg, wu, wd, layer):
    m = x2d.shape[0]
    tiles = m // FFN_TILE_M
    tile = lambda i: jnp.maximum(i - FFN_W_STEPS, 0)
    slab = lambda i: jnp.minimum(i, FFN_W_STEPS - 1)
    return pl.pallas_call(
        _ffn_kernel,
        out_shape=jax.ShapeDtypeStruct((m, D_MODEL), F32),
        grid=(FFN_W_STEPS + tiles,),
        in_specs=[
            pl.BlockSpec((FFN_TILE_M, D_MODEL), lambda i: (tile(i), 0)),
            _resident((1, D_MODEL)),
            pl.BlockSpec((None, D_MODEL, MXU_WIDTH), lambda i: (layer, 0, slab(i))),
            pl.BlockSpec((None, D_MODEL, MXU_WIDTH), lambda i: (layer, 0, slab(i))),
            pl.BlockSpec((None, MXU_WIDTH, D_MODEL), lambda i: (layer, slab(i), 0)),
        ],
        out_specs=pl.BlockSpec((FFN_TILE_M, D_MODEL), lambda i: (tile(i), 0)),
        scratch_shapes=[
            pltpu.VMEM((D_MODEL, D_FF), BF16),
            pltpu.VMEM((D_MODEL, D_FF), BF16),
            pltpu.VMEM((D_FF, D_MODEL), BF16),
        ],
        compiler_params=pltpu.CompilerParams(
            dimension_semantics=("arbitrary",), vmem_limit_bytes=VMEM_LIMIT_BYTES),
        name="swiglu_ffn",
    )(x2d, gain, wg, wu, wd)


MIX_W_STEPS = IN_MAIN_W // MXU_WIDTH
MIX_TILES_PER_SEQ = SEQ // MIX_TILE_T


def _mixer_kernel(x_ref, gain_ref, w_in_t_f32_ref, w_lr_t_f32_ref, qg_ref, kg_ref, sink_ref, svg_ref,
                  sgu_w_ref, sgu_b_ref, gup_ref, gb_ref, og_ref, w_out_f32_ref,
                  o_ref, w_in_t_s, w_lr_t_s, w_out_s, *scratch):
    i = pl.program_id(0)

    @pl.when(i < MIX_W_STEPS)
    def _():
        for c in range(MIX_W_STEPS):
            @pl.when(i == c)
            def _(c=c):
                slab = slice(c * MXU_WIDTH, (c + 1) * MXU_WIDTH)
                w_in_t_s[slab, :] = w_in_t_f32_ref[...].astype(BF16)
                if slab.stop <= MIX_W:
                    w_out_s[slab, :] = w_out_f32_ref[...].astype(BF16)
                if c == 0:
                    w_lr_t_s[...] = w_lr_t_f32_ref[...].astype(BF16)

    @pl.when(i >= MIX_W_STEPS)
    def _():
        _mixer_token_step((i - MIX_W_STEPS) % MIX_TILES_PER_SEQ, x_ref, gain_ref, w_in_t_s, w_lr_t_s,
                          qg_ref, kg_ref, sink_ref, svg_ref, sgu_w_ref, sgu_b_ref, gup_ref, gb_ref,
                          og_ref, w_out_s, o_ref, *scratch)


def _mixer_token_step(ti, x_ref, gain_ref, w_in_t_ref, w_lr_t_ref, qg_ref, kg_ref, sink_ref, svg_ref,
                      sgu_w_ref, sgu_b_ref, gup_ref, gb_ref, og_ref, w_out_ref,
                      o_ref, kbuf, vbuf, state, band_bias):
    tt = MIX_TILE_T
    rows = ATTN_GROUP * WINDOW
    kj = lax.broadcasted_iota(jnp.int32, (rows, 2 * WINDOW), 1)

    @pl.when(ti > 0)
    def _():
        kbuf[0:WINDOW, :] = kbuf[tt:tt + WINDOW, :]
        vbuf[0:WINDOW, :] = vbuf[tt:tt + WINDOW, :]

    @pl.when(ti == 0)
    def _():
        kbuf[0:WINDOW, :] = jnp.zeros((WINDOW, 2 * GLA_W), BF16)
        vbuf[0:WINDOW, :] = jnp.zeros((WINDOW, 2 * GLA_W), BF16)
        state[...] = jnp.zeros_like(state)
        rel = lax.broadcasted_iota(jnp.int32, (rows, 2 * WINDOW), 0) % WINDOW + WINDOW - kj
        band_bias[...] = jnp.where((rel >= 0) & (rel < WINDOW), 0.0, NEG_INF)

    x = x_ref[...]
    h = _rms_rows(x, gain_ref[...]).astype(BF16)
    p_attn = _dot_nt(h, w_in_t_ref[OFF_AQ:OFF_SU, :])
    lr = _dot_nt(h, w_lr_t_ref[...])
    env = {}

    def project(name, lo):
        env[name] = _dot_nt(h, w_in_t_ref[lo:lo + MXU_WIDTH, :])

    same256 = (_group_ids((GLA_W, GLA_W), 0, HEAD_DIM)
               == _group_ids((GLA_W, GLA_W), 1, HEAD_DIM))
    ones256 = jnp.where(same256, 1.0, 0.0).astype(BF16)
    lane_grp = _group_ids((1, GLA_W), 1, HEAD_DIM)
    scale = HEAD_DIM ** -0.5

    C = GLA_CHUNK
    chunks = [slice(n * C, (n + 1) * C) for n in range(tt // C)]
    logits = _dot(lr.astype(BF16), gup_ref[...]) + gb_ref[...]
    g_log = -(jnp.maximum(-logits, 0.0) + jnp.log(1.0 + jnp.exp(-jnp.abs(logits)))) * (1.0 / GLA_TAU)
    g_hi = g_log.astype(BF16)
    g_lo = (g_log - g_hi.astype(F32)).astype(BF16)

    q_slabs = []
    for c in range(ATTN_KV_HEADS):
        qs = p_attn[:, OFF_AQ + c * GLA_W: OFF_AQ + (c + 1) * GLA_W]
        q_slabs.append(_group_rms(qs, qg_ref[...], ones256) * (scale * LOG2_E))
    kn = _group_rms(p_attn[:, OFF_AK:OFF_AK + ATTN_KV_W], kg_ref[...],
                    ones256[:ATTN_KV_W, :ATTN_KV_W])

    project("su", OFF_SU)
    project("sv", OFF_SV)

    tri = jnp.where(lax.broadcasted_iota(jnp.int32, (C, C), 1)
                    <= lax.broadcasted_iota(jnp.int32, (C, C), 0), 1.0, 0.0).astype(BF16)
    b = jnp.concatenate([_dot(tri, g_hi[r, :]) + _dot(tri, g_lo[r, :]) for r in chunks], axis=0)
    b_last = [b[r.stop - 1:r.stop, :] for r in chunks]
    b_end = jnp.concatenate([jnp.broadcast_to(bl, (C, GLA_W)) for bl in b_last], axis=0)
    decay_q = jnp.exp(b)
    decay_k_intra = jnp.exp(-b)
    decay_k_state = jnp.exp(b_end - b)
    decay_state = [jnp.exp(bl) for bl in b_last]

    low_half = lax.broadcasted_iota(jnp.int32, (1, ATTN_KV_W), 1) < HEAD_DIM

    def replicate(a):
        swapped = pltpu.roll(a, HEAD_DIM, axis=1)
        h0 = jnp.where(low_half, a, swapped).astype(BF16)
        h1 = jnp.where(low_half, swapped, a).astype(BF16)
        return jnp.concatenate([h0, h0, h1, h1], axis=1)

    kbuf[WINDOW:WINDOW + tt, :] = replicate(kn)
    vbuf[WINDOW:WINDOW + tt, :] = replicate(p_attn[:, OFF_AV:OFF_AV + ATTN_KV_W])

    row_head = _group_ids((rows, 1), 0, WINDOW)
    sink_cols = []
    for c in range(ATTN_KV_HEADS):
        sink_col = jnp.zeros((rows, 1), F32)
        for g in range(ATTN_GROUP):
            sink_col = jnp.where(row_head == g, sink_ref[ATTN_GROUP * c + g] * LOG2_E, sink_col)
        sink_cols.append(sink_col)
    first_key = jnp.where(ti > 0, 0, WINDOW)

    def scores(c, j):
        qblk = q_slabs[c][j * WINDOW:(j + 1) * WINDOW, :]
        qstack = jnp.concatenate(
            [jnp.where(lane_grp == g, qblk, 0.0) for g in range(ATTN_GROUP)],
            axis=0).astype(BF16)
        k2 = kbuf[j * WINDOW:(j + 2) * WINDOW, c * GLA_W:(c + 1) * GLA_W]
        s = _dot_nt(qstack, k2) + band_bias[...]
        if j == 0:
            s = jnp.where(kj >= first_key, s, NEG_INF)
        return s

    def attend(c, j, s):
        m = jnp.maximum(jnp.max(s, axis=-1, keepdims=True), sink_cols[c])
        e = jnp.exp2(s - m)
        den = jnp.sum(e, axis=-1, keepdims=True) + jnp.exp2(sink_cols[c] - m)
        v2 = vbuf[j * WINDOW:(j + 2) * WINDOW, c * GLA_W:(c + 1) * GLA_W]
        o = _dot(e.astype(BF16), v2) * (1.0 / den)
        out = jnp.zeros((WINDOW, GLA_W), F32)
        for g in range(ATTN_GROUP):
            out = jnp.where(lane_grp == g, o[g * WINDOW:(g + 1) * WINDOW, :], out)
        return out.astype(BF16)

    def gating_u():
        env["u"] = _gelu_tanh(env["su"])

    def gating_v():
        vn = _group_rms(_gelu_tanh(env["sv"]), svg_ref[...], ones256)
        wt = lax.broadcasted_iota(jnp.int32, (SGU_CHUNK, SGU_GROUPS * SGU_CHUNK), 0)
        ws = lax.broadcasted_iota(jnp.int32, (SGU_CHUNK, SGU_GROUPS * SGU_CHUNK), 1) % SGU_CHUNK
        w_causal = jnp.where(ws <= wt, sgu_w_ref[...], 0.0).astype(BF16)
        out_b = []
        for j in range(tt // SGU_CHUNK):
            vchunk = vn[j * SGU_CHUNK:(j + 1) * SGU_CHUNK, :]
            vblk = jnp.concatenate(
                [jnp.where(lane_grp == g, vchunk, 0.0) for g in range(SGU_GROUPS)],
                axis=0).astype(BF16)
            sg = _dot(w_causal, vblk) + sgu_b_ref[...]
            out_b.append((env["u"][j * SGU_CHUNK:(j + 1) * SGU_CHUNK, :] * sg).astype(BF16))
        env["out_b"] = jnp.concatenate(out_b, axis=0)

    def gla_q():
        qd_f = (env["cq"] * scale) * decay_q
        env["qd"] = qd_f.astype(BF16)
        env["qexp"] = [
            jnp.concatenate([jnp.where(lane_grp == hh, qd_f[r, :], 0.0) for hh in range(GLA_HEADS)],
                            axis=0).astype(BF16) for r in chunks]

    def gla_k():
        ki = (env["ck"] * decay_k_intra).astype(BF16)
        env["ks"] = (env["ck"] * decay_k_state).astype(BF16)
        env["attn"] = [_dot_nt(env["qexp"][n], ki[r, :]) for n, r in enumerate(chunks)]

    def gla_v():
        cv = env["cv"]
        env["vc"] = cv.astype(BF16)
        env["delta"] = [
            jnp.where(same256, _dot(cv[r, :].T.astype(BF16), env["ks"][r, :]), 0.0)
            for r in chunks]

    def gla_intra():
        hrows = GLA_HEADS * C
        causal = (lax.broadcasted_iota(jnp.int32, (hrows, C), 1)
                  <= lax.broadcasted_iota(jnp.int32, (hrows, C), 0) % C)
        o_intra = []
        for n, r in enumerate(chunks):
            rfull = _dot(jnp.where(causal, env["attn"][n], 0.0).astype(BF16), env["vc"][r, :])
            oi = jnp.zeros((C, GLA_W), F32)
            for hh in range(GLA_HEADS):
                oi = jnp.where(lane_grp == hh, rfull[hh * C:(hh + 1) * C, :], oi)
            o_intra.append(oi)
        env["o_intra"] = o_intra

    def gla_recurrence():
        st = state[...]
        o_chunks = []
        for n, r in enumerate(chunks):
            o_chunks.append(env["o_intra"][n] + _dot_nt(env["qd"][r, :], st.astype(BF16)))
            st = st * decay_state[n] + env["delta"][n]
        state[...] = st
        env["oc"] = jnp.concatenate(o_chunks, axis=0)

    order = [(c, j) for c in range(ATTN_KV_HEADS) for j in range(tt // WINDOW)]
    blocks = [[] for _ in range(ATTN_KV_HEADS)]

    def out_proj(name, piece, lo):
        env[name] = _dot(piece, w_out_ref[lo:lo + MXU_WIDTH, :])

    between_blocks = [
        [lambda: project("cq", OFF_CQ), gating_u],
        [lambda: project("ck", OFF_CK), gating_v],
        [lambda: project("cv", OFF_CV), gla_q],
        [lambda: project("cg", OFF_CG), gla_k],
        [gla_v, lambda: out_proj("y_b", env["out_b"], ATTN_Q_W)],
        [gla_intra, lambda: out_proj("y_a0", jnp.concatenate(blocks[0], axis=0), 0)],
        [gla_recurrence],
        [],
    ]
    stride = len(order) // len(between_blocks)
    s_next = scores(*order[0])
    for i, (c, j) in enumerate(order):
        s_cur = s_next
        if i + 1 < len(order):
            s_next = scores(*order[i + 1])
        if i % stride == 0:
            for work in between_blocks[i // stride]:
                work()
        blocks[c].append(attend(c, j, s_cur))
    out_proj("y_a1", jnp.concatenate(blocks[1], axis=0), GLA_W)
    cgate = env["cg"]
    oc = _group_rms(env["oc"], og_ref[...], ones256) * (cgate * _sigmoid(cgate))
    out_proj("y_c", oc.astype(BF16), ATTN_Q_W + SGU_W)
    o_ref[...] = x + env["y_a0"] + env["y_b"] + env["y_a1"] + env["y_c"]


def _mixer(x, gain, w_in_t, qg, kg, sinks, svg, sgu_w, sgu_b, gup, gb, og, w_out, layer):
    B, T, _ = x.shape
    tt = MIX_TILE_T
    assert T // tt == MIX_TILES_PER_SEQ
    tile = lambda i: jnp.maximum(i - MIX_W_STEPS, 0)
    x_block = pl.BlockSpec((None, tt, D_MODEL),
                           lambda i: (tile(i) // MIX_TILES_PER_SEQ, tile(i) % MIX_TILES_PER_SEQ, 0))
    vmem = [
        _resident((1, D_MODEL)),
        pl.BlockSpec((None, MXU_WIDTH, D_MODEL),
                     lambda i: (layer, jnp.minimum(i, MIX_W_STEPS - 1), 0)),
        pl.BlockSpec((None, GLA_GATE_RANK, D_MODEL),
                     lambda i: (layer, IN_MAIN_W // GLA_GATE_RANK, 0), pipeline_mode=pl.Buffered(1)),
        _resident((1, GLA_W)),
        _resident((1, ATTN_KV_W)),
        pl.BlockSpec(memory_space=pltpu.SMEM),
        _resident((1, SGU_W)),
        _resident((SGU_CHUNK, SGU_GROUPS * SGU_CHUNK)),
        _resident((SGU_CHUNK, SGU_W)),
        _resident((GLA_GATE_RANK, GLA_W)),
        _resident((1, GLA_W)),
        _resident((1, GLA_W)),
        pl.BlockSpec((None, MXU_WIDTH, D_MODEL),
                     lambda i: (layer, jnp.minimum(i, MIX_W // MXU_WIDTH - 1), 0)),
    ]
    return pl.pallas_call(
        _mixer_kernel,
        out_shape=jax.ShapeDtypeStruct((B, T, D_MODEL), F32),
        grid=(MIX_W_STEPS + B * MIX_TILES_PER_SEQ,),
        in_specs=[x_block] + vmem,
        out_specs=x_block,
        scratch_shapes=[
            pltpu.VMEM((IN_MAIN_W, D_MODEL), BF16),
            pltpu.VMEM((GLA_GATE_RANK, D_MODEL), BF16),
            pltpu.VMEM((MIX_W, D_MODEL), BF16),
            pltpu.VMEM((WINDOW + tt, 2 * GLA_W), BF16),
            pltpu.VMEM((WINDOW + tt, 2 * GLA_W), BF16),
            pltpu.VMEM((GLA_W, GLA_W), F32),
            pltpu.VMEM((ATTN_GROUP * WINDOW, 2 * WINDOW), F32),
        ],
        compiler_params=pltpu.CompilerParams(
            dimension_semantics=("arbitrary",), vmem_limit_bytes=VMEM_LIMIT_BYTES),
        name="token_mixer",
    )(x, gain, w_in_t, w_in_t, qg, kg, sinks, svg, sgu_w, sgu_b, gup, gb, og, w_out)


def kernel(x, ffn1_norm, ffn1_w_gate, ffn1_w_up, ffn1_w_down, mix_norm, w_in, attn_q_norm,
           attn_k_norm, attn_sinks, sgu_v_norm, sgu_w, sgu_b, gla_w_gate_up, gla_b_gate,
           gla_out_norm, w_out, ffn2_norm, ffn2_w_gate, ffn2_w_up, ffn2_w_down):
    B, T, D = x.shape
    row = lambda v: v.reshape(1, -1).astype(F32)
    w_in_t = jnp.swapaxes(w_in, 1, 2)
    for l in range(DEPTH):
        x = _ffn(x.reshape(B * T, D), row(ffn1_norm[l]), ffn1_w_gate, ffn1_w_up,
                 ffn1_w_down, l).reshape(B, T, D)
        x = _mixer(
            x, row(mix_norm[l]),
            w_in_t,
            row(jnp.tile(attn_q_norm[l], ATTN_GROUP)), row(jnp.tile(attn_k_norm[l], ATTN_KV_HEADS)),
            attn_sinks[l].astype(F32), row(sgu_v_norm[l]),
            jnp.transpose(sgu_w[l], (1, 0, 2)).reshape(SGU_CHUNK, SGU_GROUPS * SGU_CHUNK).astype(F32),
            jnp.repeat(sgu_b[l].T, HEAD_DIM, axis=1).astype(F32),
            gla_w_gate_up[l].astype(BF16), row(gla_b_gate[l]),
            row(jnp.tile(gla_out_norm[l], GLA_HEADS)), w_out, l)
        x = _ffn(x.reshape(B * T, D), row(ffn2_norm[l]), ffn2_w_gate, ffn2_w_up,
                 ffn2_w_down, l).reshape(B, T, D)
    return x
```

```python
import jax
import jax.numpy as jnp
from jax import lax
from jax.experimental import pallas as pl
from jax.experimental.pallas import tpu as pltpu

F32 = jnp.float32
BF16 = jnp.bfloat16

D_MODEL = 1024
D_FF = 2816
SEQ = 4096
DEPTH = 2
HEAD_DIM = 64
ATTN_Q_HEADS = 8
ATTN_KV_HEADS = 2
ATTN_GROUP = ATTN_Q_HEADS // ATTN_KV_HEADS
WINDOW = 128
SGU_GROUPS = 4
SGU_CHUNK = 128
GLA_HEADS = 4
GLA_GATE_RANK = 16
GLA_TAU = 16.0
GLA_CHUNK = 64
RMS_EPS = 1e-6
NEG_INF = -1e30
LOG2_E = 1.4426950408889634

ATTN_Q_W = ATTN_Q_HEADS * HEAD_DIM
ATTN_KV_W = ATTN_KV_HEADS * HEAD_DIM
SGU_W = SGU_GROUPS * HEAD_DIM
GLA_W = GLA_HEADS * HEAD_DIM
OFF_AQ = 0
OFF_AK = OFF_AQ + ATTN_Q_W
OFF_AV = OFF_AK + ATTN_KV_W
OFF_SU = OFF_AV + ATTN_KV_W
OFF_SV = OFF_SU + SGU_W
OFF_CQ = OFF_SV + SGU_W
OFF_CK = OFF_CQ + GLA_W
OFF_CV = OFF_CK + GLA_W
OFF_CG = OFF_CV + GLA_W
OFF_LR = OFF_CG + GLA_W
IN_MAIN_W = OFF_LR
MIX_W = ATTN_Q_W + SGU_W + GLA_W

FFN_TILE_M = 1024
MXU_WIDTH = 256
LANES = 128
FFN_CHUNK_BOUNDS = ((0, 4 * MXU_WIDTH), (4 * MXU_WIDTH, 8 * MXU_WIDTH), (8 * MXU_WIDTH, D_FF))
MIX_TILE_T = 512
VMEM_LIMIT_BYTES = 56 * 1024 * 1024


def _dot(a, b):
    return jnp.dot(a, b, preferred_element_type=F32)


def _dot_nt(a, b):
    return lax.dot_general(a, b, (((1,), (1,)), ((), ())), preferred_element_type=F32)


def _sigmoid(x):
    return 1.0 / (1.0 + jnp.exp(-x))


def _gelu_tanh(x):
    c = 0.7978845608028654
    return x * (0.5 * (1.0 + jnp.tanh(c * (x + 0.044715 * (x * x * x)))))


def _rms_rows(x, gain):
    ms = jnp.mean(x * x, axis=-1, keepdims=True)
    return x * lax.rsqrt(ms + RMS_EPS) * gain


def _group_ids(shape, dim, group):
    return lax.broadcasted_iota(jnp.int32, shape, dim) // group


def _group_rms(x, gain):
    low = lax.broadcasted_iota(jnp.int32, (1, LANES), 1) < HEAD_DIM
    sq = x * x
    out = []
    for s in range(0, x.shape[1], LANES):
        t = sq[:, s:s + LANES]
        ss_low = jnp.sum(jnp.where(low, t, 0.0), axis=-1, keepdims=True)
        ss_high = jnp.sum(jnp.where(low, 0.0, t), axis=-1, keepdims=True)
        ms = jnp.where(low, ss_low, ss_high) * (1.0 / HEAD_DIM)
        out.append(x[:, s:s + LANES] * lax.rsqrt(ms + RMS_EPS))
    return jnp.concatenate(out, axis=1) * gain


FFN_W_STEPS = D_FF // MXU_WIDTH


def _ffn_kernel(x_ref, gain_ref, wg_ref, wu_ref, wd_ref, o_ref, wg_s, wu_s, wd_s):
    i = pl.program_id(0)

    @pl.when(i < FFN_W_STEPS)
    def _():
        for c in range(FFN_W_STEPS):
            @pl.when(i == c)
            def _(c=c):
                slab = slice(c * MXU_WIDTH, (c + 1) * MXU_WIDTH)
                wg_s[:, slab] = wg_ref[...].astype(BF16)
                wu_s[:, slab] = wu_ref[...].astype(BF16)
                wd_s[slab, :] = wd_ref[...].astype(BF16)

    @pl.when(i >= FFN_W_STEPS)
    def _():
        x = x_ref[...]
        h = _rms_rows(x, gain_ref[...]).astype(BF16)
        y = None
        for lo, hi in FFN_CHUNK_BOUNDS:
            gate = _dot(h, wg_s[:, lo:hi])
            up = _dot(h, wu_s[:, lo:hi])
            act = (gate * _sigmoid(gate) * up).astype(BF16)
            part = _dot(act, wd_s[lo:hi, :])
            y = part if y is None else y + part
        o_ref[...] = x + 0.5 * y


def _resident(shape):
    nd = len(shape)
    return pl.BlockSpec(shape, lambda *_: (0,) * nd, pipeline_mode=pl.Buffered(1))


def _ffn(x2d, gain, wg, wu, wd, layer):
    m = x2d.shape[0]
    tiles = m // FFN_TILE_M
    tile = lambda i: jnp.maximum(i - FFN_W_STEPS, 0)
    slab = lambda i: jnp.minimum(i, FFN_W_STEPS - 1)
    return pl.pallas_call(
        _ffn_kernel,
        out_shape=jax.ShapeDtypeStruct((m, D_MODEL), F32),
        grid=(FFN_W_STEPS + tiles,),
        in_specs=[
            pl.BlockSpec((FFN_TILE_M, D_MODEL), lambda i: (tile(i), 0)),
            _resident((1, D_MODEL)),
            pl.BlockSpec((None, D_MODEL, MXU_WIDTH), lambda i: (layer, 0, slab(i))),
            pl.BlockSpec((None, D_MODEL, MXU_WIDTH), lambda i: (layer, 0, slab(i))),
            pl.BlockSpec((None, MXU_WIDTH, D_MODEL), lambda i: (layer, slab(i), 0)),
        ],
        out_specs=pl.BlockSpec((FFN_TILE_M, D_MODEL), lambda i: (tile(i), 0)),
        scratch_shapes=[
            pltpu.VMEM((D_MODEL, D_FF), BF16),
            pltpu.VMEM((D_MODEL, D_FF), BF16),
            pltpu.VMEM((D_FF, D_MODEL), BF16),
        ],
        compiler_params=pltpu.CompilerParams(
            dimension_semantics=("arbitrary",), vmem_limit_bytes=VMEM_LIMIT_BYTES),
        name="swiglu_ffn",
    )(x2d, gain, wg, wu, wd)


MIX_W_STEPS = IN_MAIN_W // MXU_WIDTH
MIX_TILES_PER_SEQ = SEQ // MIX_TILE_T


def _mixer_kernel(x_ref, gain_ref, w_in_t_f32_ref, w_lr_t_f32_ref, qg_ref, kg_ref, sink_ref, svg_ref,
                  sgu_w_ref, sgu_b_ref, gup_ref, gb_ref, og_ref, w_out_f32_ref,
                  o_ref, w_in_t_s, w_lr_t_s, w_out_s, *scratch):
    i = pl.program_id(0)

    @pl.when(i < MIX_W_STEPS)
    def _():
        for c in range(MIX_W_STEPS):
            @pl.when(i == c)
            def _(c=c):
                slab = slice(c * MXU_WIDTH, (c + 1) * MXU_WIDTH)
                w_in_t_s[slab, :] = w_in_t_f32_ref[...].astype(BF16)
                if slab.stop <= MIX_W:
                    w_out_s[slab, :] = w_out_f32_ref[...].astype(BF16)
                if c == 0:
                    w_lr_t_s[...] = w_lr_t_f32_ref[...].astype(BF16)

    @pl.when(i >= MIX_W_STEPS)
    def _():
        _mixer_token_step((i - MIX_W_STEPS) % MIX_TILES_PER_SEQ, x_ref, gain_ref, w_in_t_s, w_lr_t_s,
                          qg_ref, kg_ref, sink_ref, svg_ref, sgu_w_ref, sgu_b_ref, gup_ref, gb_ref,
                          og_ref, w_out_s, o_ref, *scratch)


def _mixer_token_step(ti, x_ref, gain_ref, w_in_t_ref, w_lr_t_ref, qg_ref, kg_ref, sink_ref, svg_ref,
                      sgu_w_ref, sgu_b_ref, gup_ref, gb_ref, og_ref, w_out_ref,
                      o_ref, kbuf, vbuf, state, band_bias):
    tt = MIX_TILE_T
    rows = ATTN_GROUP * WINDOW
    kj = lax.broadcasted_iota(jnp.int32, (rows, 2 * WINDOW), 1)

    @pl.when(ti > 0)
    def _():
        kbuf[0:WINDOW, :] = kbuf[tt:tt + WINDOW, :]
        vbuf[0:WINDOW, :] = vbuf[tt:tt + WINDOW, :]

    @pl.when(ti == 0)
    def _():
        kbuf[0:WINDOW, :] = jnp.zeros((WINDOW, 2 * GLA_W), BF16)
        vbuf[0:WINDOW, :] = jnp.zeros((WINDOW, 2 * GLA_W), BF16)
        state[...] = jnp.zeros_like(state)
        rel = lax.broadcasted_iota(jnp.int32, (rows, 2 * WINDOW), 0) % WINDOW + WINDOW - kj
        band_bias[...] = jnp.where((rel >= 0) & (rel < WINDOW), 0.0, NEG_INF)

    x = x_ref[...]
    h = _rms_rows(x, gain_ref[...]).astype(BF16)
    p_attn = _dot_nt(h, w_in_t_ref[OFF_AQ:OFF_SU, :])
    lr = _dot_nt(h, w_lr_t_ref[...])
    env = {}

    def project(name, lo):
        env[name] = _dot_nt(h, w_in_t_ref[lo:lo + MXU_WIDTH, :])

    same256 = (_group_ids((GLA_W, GLA_W), 0, HEAD_DIM)
               == _group_ids((GLA_W, GLA_W), 1, HEAD_DIM))
    lane_grp = _group_ids((1, GLA_W), 1, HEAD_DIM)
    scale = HEAD_DIM ** -0.5

    C = GLA_CHUNK
    chunks = [slice(n * C, (n + 1) * C) for n in range(tt // C)]
    logits = _dot(lr.astype(BF16), gup_ref[...]) + gb_ref[...]
    g_log = -(jnp.maximum(-logits, 0.0) + jnp.log(1.0 + jnp.exp(-jnp.abs(logits)))) * (1.0 / GLA_TAU)
    g_hi = g_log.astype(BF16)
    g_lo = (g_log - g_hi.astype(F32)).astype(BF16)

    q_slabs = []
    for c in range(ATTN_KV_HEADS):
        qs = p_attn[:, OFF_AQ + c * GLA_W: OFF_AQ + (c + 1) * GLA_W]
        q_slabs.append(_group_rms(qs, qg_ref[...]) * (scale * LOG2_E))
    kn = _group_rms(p_attn[:, OFF_AK:OFF_AK + ATTN_KV_W], kg_ref[...])

    project("su", OFF_SU)
    project("sv", OFF_SV)

    tri = jnp.where(lax.broadcasted_iota(jnp.int32, (C, C), 1)
                    <= lax.broadcasted_iota(jnp.int32, (C, C), 0), 1.0, 0.0).astype(BF16)
    b = jnp.concatenate([_dot(tri, g_hi[r, :]) + _dot(tri, g_lo[r, :]) for r in chunks], axis=0)
    b_last = [b[r.stop - 1:r.stop, :] for r in chunks]
    b_end = jnp.concatenate([jnp.broadcast_to(bl, (C, GLA_W)) for bl in b_last], axis=0)
    decay_q = jnp.exp(b)
    decay_k_intra = jnp.exp(-b)
    decay_k_state = jnp.exp(b_end - b)
    decay_state = [jnp.exp(bl) for bl in b_last]

    low_half = lax.broadcasted_iota(jnp.int32, (1, ATTN_KV_W), 1) < HEAD_DIM

    def replicate(a):
        swapped = pltpu.roll(a, HEAD_DIM, axis=1)
        h0 = jnp.where(low_half, a, swapped).astype(BF16)
        h1 = jnp.where(low_half, swapped, a).astype(BF16)
        return jnp.concatenate([h0, h0, h1, h1], axis=1)

    kbuf[WINDOW:WINDOW + tt, :] = replicate(kn)
    vbuf[WINDOW:WINDOW + tt, :] = replicate(p_attn[:, OFF_AV:OFF_AV + ATTN_KV_W])

    row_head = _group_ids((rows, 1), 0, WINDOW)
    sink_cols = []
    for c in range(ATTN_KV_HEADS):
        sink_col = jnp.zeros((rows, 1), F32)
        for g in range(ATTN_GROUP):
            sink_col = jnp.where(row_head == g, sink_ref[ATTN_GROUP * c + g] * LOG2_E, sink_col)
        sink_cols.append(sink_col)
    first_key = jnp.where(ti > 0, 0, WINDOW)

    def scores(c, j):
        qblk = q_slabs[c][j * WINDOW:(j + 1) * WINDOW, :]
        qstack = jnp.concatenate(
            [jnp.where(lane_grp == g, qblk, 0.0) for g in range(ATTN_GROUP)],
            axis=0).astype(BF16)
        k2 = kbuf[j * WINDOW:(j + 2) * WINDOW, c * GLA_W:(c + 1) * GLA_W]
        s = _dot_nt(qstack, k2) + band_bias[...]
        if j == 0:
            s = jnp.where(kj >= first_key, s, NEG_INF)
        return s

    def attend(c, j, s):
        m = jnp.maximum(jnp.max(s, axis=-1, keepdims=True), sink_cols[c])
        e = jnp.exp2(s - m)
        den = jnp.sum(e, axis=-1, keepdims=True) + jnp.exp2(sink_cols[c] - m)
        v2 = vbuf[j * WINDOW:(j + 2) * WINDOW, c * GLA_W:(c + 1) * GLA_W]
        o = _dot(e.astype(BF16), v2) * (1.0 / den)
        out = jnp.zeros((WINDOW, GLA_W), F32)
        for g in range(ATTN_GROUP):
            out = jnp.where(lane_grp == g, o[g * WINDOW:(g + 1) * WINDOW, :], out)
        return out.astype(BF16)

    def gating_u():
        env["u"] = _gelu_tanh(env["su"])

    def gating_v():
        vn = _group_rms(_gelu_tanh(env["sv"]), svg_ref[...])
        wt = lax.broadcasted_iota(jnp.int32, (SGU_CHUNK, SGU_GROUPS * SGU_CHUNK), 0)
        ws = lax.broadcasted_iota(jnp.int32, (SGU_CHUNK, SGU_GROUPS * SGU_CHUNK), 1) % SGU_CHUNK
        w_causal = jnp.where(ws <= wt, sgu_w_ref[...], 0.0).astype(BF16)
        out_b = []
        for j in range(tt // SGU_CHUNK):
            vchunk = vn[j * SGU_CHUNK:(j + 1) * SGU_CHUNK, :]
            vblk = jnp.concatenate(
                [jnp.where(lane_grp == g, vchunk, 0.0) for g in range(SGU_GROUPS)],
                axis=0).astype(BF16)
            sg = _dot(w_causal, vblk) + sgu_b_ref[...]
            out_b.append((env["u"][j * SGU_CHUNK:(j + 1) * SGU_CHUNK, :] * sg).astype(BF16))
        env["out_b"] = jnp.concatenate(out_b, axis=0)

    def gla_q():
        qd_f = (env["cq"] * scale) * decay_q
        env["qd"] = qd_f.astype(BF16)
        env["qexp"] = [
            jnp.concatenate([jnp.where(lane_grp == hh, qd_f[r, :], 0.0) for hh in range(GLA_HEADS)],
                            axis=0).astype(BF16) for r in chunks]

    def gla_k():
        ki = (env["ck"] * decay_k_intra).astype(BF16)
        env["ks"] = (env["ck"] * decay_k_state).astype(BF16)
        env["attn"] = [_dot_nt(env["qexp"][n], ki[r, :]) for n, r in enumerate(chunks)]

    def gla_v():
        cv = env["cv"]
        env["vc"] = cv.astype(BF16)
        env["delta"] = [
            jnp.where(same256, _dot(cv[r, :].T.astype(BF16), env["ks"][r, :]), 0.0)
            for r in chunks]

    def gla_intra():
        hrows = GLA_HEADS * C
        causal = (lax.broadcasted_iota(jnp.int32, (hrows, C), 1)
                  <= lax.broadcasted_iota(jnp.int32, (hrows, C), 0) % C)
        o_intra = []
        for n, r in enumerate(chunks):
            rfull = _dot(jnp.where(causal, env["attn"][n], 0.0).astype(BF16), env["vc"][r, :])
            oi = jnp.zeros((C, GLA_W), F32)
            for hh in range(GLA_HEADS):
                oi = jnp.where(lane_grp == hh, rfull[hh * C:(hh + 1) * C, :], oi)
            o_intra.append(oi)
        env["o_intra"] = o_intra

    order = [(c, j) for c in range(ATTN_KV_HEADS) for j in range(tt // WINDOW)]
    blocks = [[] for _ in range(ATTN_KV_HEADS)]

    def out_proj(name, piece, lo):
        env[name] = _dot(piece, w_out_ref[lo:lo + MXU_WIDTH, :])

    def gla_recurrence():
        st = state[...]
        o_chunks = []
        for n, r in enumerate(chunks):
            o_chunks.append(env["o_intra"][n] + _dot_nt(env["qd"][r, :], st.astype(BF16)))
            st = st * decay_state[n] + env["delta"][n]
        state[...] = st
        env["oc"] = jnp.concatenate(o_chunks, axis=0)

    def gla_out():
        cgate = env["cg"]
        oc = _group_rms(env["oc"], og_ref[...]) * (cgate * _sigmoid(cgate))
        out_proj("y_c", oc.astype(BF16), ATTN_Q_W + SGU_W)

    between_blocks = [
        [lambda: project("cq", OFF_CQ), gating_u],
        [lambda: project("ck", OFF_CK), gating_v],
        [lambda: project("cv", OFF_CV), gla_q],
        [lambda: project("cg", OFF_CG), gla_k],
        [gla_v, gla_intra],
        [gla_recurrence, lambda: out_proj("y_b", env["out_b"], ATTN_Q_W)],
        [lambda: out_proj("y_a0", jnp.concatenate(blocks[0], axis=0), 0), gla_out],
        [],
    ]
    stride = len(order) // len(between_blocks)
    s_next = scores(*order[0])
    for i, (c, j) in enumerate(order):
        s_cur = s_next
        if i + 1 < len(order):
            s_next = scores(*order[i + 1])
        if i % stride == 0:
            for work in between_blocks[i // stride]:
                work()
        blocks[c].append(attend(c, j, s_cur))
    out_proj("y_a1", jnp.concatenate(blocks[1], axis=0), GLA_W)
    o_ref[...] = x + env["y_a0"] + env["y_b"] + env["y_a1"] + env["y_c"]


def _mixer(x, gain, w_in_t, qg, kg, sinks, svg, sgu_w, sgu_b, gup, gb, og, w_out, layer):
    B, T, _ = x.shape
    tt = MIX_TILE_T
    assert T // tt == MIX_TILES_PER_SEQ
    tile = lambda i: jnp.maximum(i - MIX_W_STEPS, 0)
    x_block = pl.BlockSpec((None, tt, D_MODEL),
                           lambda i: (tile(i) // MIX_TILES_PER_SEQ, tile(i) % MIX_TILES_PER_SEQ, 0))
    vmem = [
        _resident((1, D_MODEL)),
        pl.BlockSpec((None, MXU_WIDTH, D_MODEL),
                     lambda i: (layer, jnp.minimum(i, MIX_W_STEPS - 1), 0)),
        pl.BlockSpec((None, GLA_GATE_RANK, D_MODEL),
                     lambda i: (layer, IN_MAIN_W // GLA_GATE_RANK, 0), pipeline_mode=pl.Buffered(1)),
        _resident((1, GLA_W)),
        _resident((1, ATTN_KV_W)),
        pl.BlockSpec(memory_space=pltpu.SMEM),
        _resident((1, SGU_W)),
        _resident((SGU_CHUNK, SGU_GROUPS * SGU_CHUNK)),
        _resident((SGU_CHUNK, SGU_W)),
        _resident((GLA_GATE_RANK, GLA_W)),
        _resident((1, GLA_W)),
        _resident((1, GLA_W)),
        pl.BlockSpec((None, MXU_WIDTH, D_MODEL),
                     lambda i: (layer, jnp.minimum(i, MIX_W // MXU_WIDTH - 1), 0)),
    ]
    return pl.pallas_call(
        _mixer_kernel,
        out_shape=jax.ShapeDtypeStruct((B, T, D_MODEL), F32),
        grid=(MIX_W_STEPS + B * MIX_TILES_PER_SEQ,),
        in_specs=[x_block] + vmem,
        out_specs=x_block,
        scratch_shapes=[
            pltpu.VMEM((IN_MAIN_W, D_MODEL), BF16),
            pltpu.VMEM((GLA_GATE_RANK, D_MODEL), BF16),
            pltpu.VMEM((MIX_W, D_MODEL), BF16),
            pltpu.VMEM((WINDOW + tt, 2 * GLA_W), BF16),
            pltpu.VMEM((WINDOW + tt, 2 * GLA_W), BF16),
            pltpu.VMEM((GLA_W, GLA_W), F32),
            pltpu.VMEM((ATTN_GROUP * WINDOW, 2 * WINDOW), F32),
        ],
        compiler_params=pltpu.CompilerParams(
            dimension_semantics=("arbitrary",), vmem_limit_bytes=VMEM_LIMIT_BYTES),
        name="token_mixer",
    )(x, gain, w_in_t, w_in_t, qg, kg, sinks, svg, sgu_w, sgu_b, gup, gb, og, w_out)


def kernel(x, ffn1_norm, ffn1_w_gate, ffn1_w_up, ffn1_w_down, mix_norm, w_in, attn_q_norm,
           attn_k_norm, attn_sinks, sgu_v_norm, sgu_w, sgu_b, gla_w_gate_up, gla_b_gate,
           gla_out_norm, w_out, ffn2_norm, ffn2_w_gate, ffn2_w_up, ffn2_w_down):
    B, T, D = x.shape
    row = lambda v: v.reshape(1, -1).astype(F32)
    w_in_t = jnp.swapaxes(w_in, 1, 2)
    for l in range(DEPTH):
        x = _ffn(x.reshape(B * T, D), row(ffn1_norm[l]), ffn1_w_gate, ffn1_w_up,
                 ffn1_w_down, l).reshape(B, T, D)
        x = _mixer(
            x, row(mix_norm[l]),
            w_in_t,
            row(jnp.tile(attn_q_norm[l], ATTN_GROUP)), row(jnp.tile(attn_k_norm[l], ATTN_KV_HEADS)),
            attn_sinks[l].astype(F32), row(sgu_v_norm[l]),
            jnp.transpose(sgu_w[l], (1, 0, 2)).reshape(SGU_CHUNK, SGU_GROUPS * SGU_CHUNK).astype(F32),
            jnp.repeat(sgu_b[l].T, HEAD_DIM, axis=1).astype(F32),
            gla_w_gate_up[l].astype(BF16), row(gla_b_gate[l]),
            row(jnp.tile(gla_out_norm[l], GLA_HEADS)), w_out, l)
        x = _ffn(x.reshape(B * T, D), row(ffn2_norm[l]), ffn2_w_gate, ffn2_w_up,
                 ffn2_w_down, l).reshape(B, T, D)
    return x
```

```python
import jax
import jax.numpy as jnp
from jax import lax
from jax.experimental import pallas as pl
from jax.experimental.pallas import tpu as pltpu

F32 = jnp.float32
BF16 = jnp.bfloat16

D_MODEL = 1024
D_FF = 2816
SEQ = 4096
DEPTH = 2
HEAD_DIM = 64
ATTN_Q_HEADS = 8
ATTN_KV_HEADS = 2
ATTN_GROUP = ATTN_Q_HEADS // ATTN_KV_HEADS
WINDOW = 128
SGU_GROUPS = 4
SGU_CHUNK = 128
GLA_HEADS = 4
GLA_GATE_RANK = 16
GLA_TAU = 16.0
GLA_CHUNK = 64
RMS_EPS = 1e-6
NEG_INF = -1e30
LOG2_E = 1.4426950408889634

ATTN_Q_W = ATTN_Q_HEADS * HEAD_DIM
ATTN_KV_W = ATTN_KV_HEADS * HEAD_DIM
SGU_W = SGU_GROUPS * HEAD_DIM
GLA_W = GLA_HEADS * HEAD_DIM
OFF_AQ = 0
OFF_AK = OFF_AQ + ATTN_Q_W
OFF_AV = OFF_AK + ATTN_KV_W
OFF_SU = OFF_AV + ATTN_KV_W
OFF_SV = OFF_SU + SGU_W
OFF_CQ = OFF_SV + SGU_W
OFF_CK = OFF_CQ + GLA_W
OFF_CV = OFF_CK + GLA_W
OFF_CG = OFF_CV + GLA_W
OFF_LR = OFF_CG + GLA_W
IN_MAIN_W = OFF_LR
MIX_W = ATTN_Q_W + SGU_W + GLA_W

FFN_TILE_M = 1024
MXU_WIDTH = 256
LANES = 128
FFN_CHUNK_BOUNDS = ((0, 6 * MXU_WIDTH), (6 * MXU_WIDTH, D_FF))
MIX_TILE_T = 1024
VMEM_LIMIT_BYTES = 56 * 1024 * 1024


def _dot(a, b):
    return jnp.dot(a, b, preferred_element_type=F32)


def _dot_nt(a, b):
    return lax.dot_general(a, b, (((1,), (1,)), ((), ())), preferred_element_type=F32)


def _sigmoid(x):
    return 1.0 / (1.0 + jnp.exp(-x))


def _gelu_tanh(x):
    c = 0.7978845608028654
    return x * (0.5 * (1.0 + jnp.tanh(c * (x + 0.044715 * (x * x * x)))))


def _rms_rows(x, gain):
    ms = jnp.mean(x * x, axis=-1, keepdims=True)
    return x * lax.rsqrt(ms + RMS_EPS) * gain


def _group_ids(shape, dim, group):
    return lax.broadcasted_iota(jnp.int32, shape, dim) // group


def _group_rms(x, gain):
    low = lax.broadcasted_iota(jnp.int32, (1, LANES), 1) < HEAD_DIM
    sq = x * x
    out = []
    for s in range(0, x.shape[1], LANES):
        t = sq[:, s:s + LANES]
        ss_low = jnp.sum(jnp.where(low, t, 0.0), axis=-1, keepdims=True)
        ss_high = jnp.sum(jnp.where(low, 0.0, t), axis=-1, keepdims=True)
        ms = jnp.where(low, ss_low, ss_high) * (1.0 / HEAD_DIM)
        out.append(x[:, s:s + LANES] * lax.rsqrt(ms + RMS_EPS))
    return jnp.concatenate(out, axis=1) * gain


FFN_W_STEPS = D_FF // MXU_WIDTH


def _ffn_kernel(x_ref, gain_ref, wg_ref, wu_ref, wd_ref, o_ref, wg_s, wu_s, wd_s):
    i = pl.program_id(0)

    @pl.when(i < FFN_W_STEPS)
    def _():
        for c in range(FFN_W_STEPS):
            @pl.when(i == c)
            def _(c=c):
                slab = slice(c * MXU_WIDTH, (c + 1) * MXU_WIDTH)
                wg_s[:, slab] = wg_ref[...].astype(BF16)
                wu_s[:, slab] = wu_ref[...].astype(BF16)
                wd_s[slab, :] = wd_ref[...].astype(BF16)

    @pl.when(i >= FFN_W_STEPS)
    def _():
        x = x_ref[...]
        h = _rms_rows(x, gain_ref[...]).astype(BF16)
        y = None
        for lo, hi in FFN_CHUNK_BOUNDS:
            gate = _dot(h, wg_s[:, lo:hi])
            up = _dot(h, wu_s[:, lo:hi])
            act = (gate * _sigmoid(gate) * up).astype(BF16)
            part = _dot(act, wd_s[lo:hi, :])
            y = part if y is None else y + part
        o_ref[...] = x + 0.5 * y


def _resident(shape):
    nd = len(shape)
    return pl.BlockSpec(shape, lambda *_: (0,) * nd, pipeline_mode=pl.Buffered(1))


def _ffn(x2d, gain, wg, wu, wd, layer):
    m = x2d.shape[0]
    tiles = m // FFN_TILE_M
    tile = lambda i: jnp.maximum(i - FFN_W_STEPS, 0)
    slab = lambda i: jnp.minimum(i, FFN_W_STEPS - 1)
    return pl.pallas_call(
        _ffn_kernel,
        out_shape=jax.ShapeDtypeStruct((m, D_MODEL), F32),
        grid=(FFN_W_STEPS + tiles,),
        in_specs=[
            pl.BlockSpec((FFN_TILE_M, D_MODEL), lambda i: (tile(i), 0)),
            _resident((1, D_MODEL)),
            pl.BlockSpec((None, D_MODEL, MXU_WIDTH), lambda i: (layer, 0, slab(i))),
            pl.BlockSpec((None, D_MODEL, MXU_WIDTH), lambda i: (layer, 0, slab(i))),
            pl.BlockSpec((None, MXU_WIDTH, D_MODEL), lambda i: (layer, slab(i), 0)),
        ],
        out_specs=pl.BlockSpec((FFN_TILE_M, D_MODEL), lambda i: (tile(i), 0)),
        scratch_shapes=[
            pltpu.VMEM((D_MODEL, D_FF), BF16),
            pltpu.VMEM((D_MODEL, D_FF), BF16),
            pltpu.VMEM((D_FF, D_MODEL), BF16),
        ],
        compiler_params=pltpu.CompilerParams(
            dimension_semantics=("arbitrary",), vmem_limit_bytes=VMEM_LIMIT_BYTES),
        name="swiglu_ffn",
    )(x2d, gain, wg, wu, wd)


MIX_W_STEPS = IN_MAIN_W // MXU_WIDTH
MIX_TILES_PER_SEQ = SEQ // MIX_TILE_T


def _mixer_kernel(x_ref, gain_ref, w_in_t_f32_ref, w_lr_t_f32_ref, qg_ref, kg_ref, sink_ref, svg_ref,
                  sgu_w_ref, sgu_b_ref, gup_ref, gb_ref, og_ref, w_out_f32_ref,
                  o_ref, w_in_t_s, w_lr_t_s, w_out_s, *scratch):
    i = pl.program_id(0)

    @pl.when(i < MIX_W_STEPS)
    def _():
        for c in range(MIX_W_STEPS):
            @pl.when(i == c)
            def _(c=c):
                slab = slice(c * MXU_WIDTH, (c + 1) * MXU_WIDTH)
                w_in_t_s[slab, :] = w_in_t_f32_ref[...].astype(BF16)
                if slab.stop <= MIX_W:
                    w_out_s[slab, :] = w_out_f32_ref[...].astype(BF16)
                if c == 0:
                    w_lr_t_s[...] = w_lr_t_f32_ref[...].astype(BF16)

    @pl.when(i >= MIX_W_STEPS)
    def _():
        _mixer_token_step((i - MIX_W_STEPS) % MIX_TILES_PER_SEQ, x_ref, gain_ref, w_in_t_s, w_lr_t_s,
                          qg_ref, kg_ref, sink_ref, svg_ref, sgu_w_ref, sgu_b_ref, gup_ref, gb_ref,
                          og_ref, w_out_s, o_ref, *scratch)


def _mixer_token_step(ti, x_ref, gain_ref, w_in_t_ref, w_lr_t_ref, qg_ref, kg_ref, sink_ref, svg_ref,
                      sgu_w_ref, sgu_b_ref, gup_ref, gb_ref, og_ref, w_out_ref,
                      o_ref, kbuf, vbuf, state, band_bias):
    tt = MIX_TILE_T
    rows = ATTN_GROUP * WINDOW
    kj = lax.broadcasted_iota(jnp.int32, (rows, 2 * WINDOW), 1)

    @pl.when(ti > 0)
    def _():
        kbuf[0:WINDOW, :] = kbuf[tt:tt + WINDOW, :]
        vbuf[0:WINDOW, :] = vbuf[tt:tt + WINDOW, :]

    @pl.when(ti == 0)
    def _():
        kbuf[0:WINDOW, :] = jnp.zeros((WINDOW, 2 * GLA_W), BF16)
        vbuf[0:WINDOW, :] = jnp.zeros((WINDOW, 2 * GLA_W), BF16)
        state[...] = jnp.zeros_like(state)
        rel = lax.broadcasted_iota(jnp.int32, (rows, 2 * WINDOW), 0) % WINDOW + WINDOW - kj
        band_bias[...] = jnp.where((rel >= 0) & (rel < WINDOW), 0.0, NEG_INF)

    x = x_ref[...]
    h = _rms_rows(x, gain_ref[...]).astype(BF16)
    p_attn = _dot_nt(h, w_in_t_ref[OFF_AQ:OFF_SU, :])
    lr = _dot_nt(h, w_lr_t_ref[...])
    env = {}

    def project(name, lo):
        env[name] = _dot_nt(h, w_in_t_ref[lo:lo + MXU_WIDTH, :])

    same256 = (_group_ids((GLA_W, GLA_W), 0, HEAD_DIM)
               == _group_ids((GLA_W, GLA_W), 1, HEAD_DIM))
    lane_grp = _group_ids((1, GLA_W), 1, HEAD_DIM)
    scale = HEAD_DIM ** -0.5

    C = GLA_CHUNK
    chunks = [slice(n * C, (n + 1) * C) for n in range(tt // C)]
    logits = _dot(lr.astype(BF16), gup_ref[...]) + gb_ref[...]
    g_log = -(jnp.maximum(-logits, 0.0) + jnp.log(1.0 + jnp.exp(-jnp.abs(logits)))) * (1.0 / GLA_TAU)
    g_hi = g_log.astype(BF16)
    g_lo = (g_log - g_hi.astype(F32)).astype(BF16)

    q_slabs = []
    for c in range(ATTN_KV_HEADS):
        qs = p_attn[:, OFF_AQ + c * GLA_W: OFF_AQ + (c + 1) * GLA_W]
        q_slabs.append(_group_rms(qs, qg_ref[...]) * (scale * LOG2_E))
    kn = _group_rms(p_attn[:, OFF_AK:OFF_AK + ATTN_KV_W], kg_ref[...])

    project("su", OFF_SU)
    project("sv", OFF_SV)

    tri = jnp.where(lax.broadcasted_iota(jnp.int32, (C, C), 1)
                    <= lax.broadcasted_iota(jnp.int32, (C, C), 0), 1.0, 0.0).astype(BF16)
    b = jnp.concatenate([_dot(tri, g_hi[r, :]) + _dot(tri, g_lo[r, :]) for r in chunks], axis=0)
    b_last = [b[r.stop - 1:r.stop, :] for r in chunks]
    b_end = jnp.concatenate([jnp.broadcast_to(bl, (C, GLA_W)) for bl in b_last], axis=0)
    decay_q = jnp.exp(b)
    decay_k_intra = jnp.exp(-b)
    decay_k_state = jnp.exp(b_end - b)
    decay_state = [jnp.exp(bl) for bl in b_last]

    low_half = lax.broadcasted_iota(jnp.int32, (1, ATTN_KV_W), 1) < HEAD_DIM

    def replicate(a):
        swapped = pltpu.roll(a, HEAD_DIM, axis=1)
        h0 = jnp.where(low_half, a, swapped).astype(BF16)
        h1 = jnp.where(low_half, swapped, a).astype(BF16)
        return jnp.concatenate([h0, h0, h1, h1], axis=1)

    kbuf[WINDOW:WINDOW + tt, :] = replicate(kn)
    vbuf[WINDOW:WINDOW + tt, :] = replicate(p_attn[:, OFF_AV:OFF_AV + ATTN_KV_W])

    row_head = _group_ids((rows, 1), 0, WINDOW)
    sink_cols = []
    for c in range(ATTN_KV_HEADS):
        sink_col = jnp.zeros((rows, 1), F32)
        for g in range(ATTN_GROUP):
            sink_col = jnp.where(row_head == g, sink_ref[ATTN_GROUP * c + g] * LOG2_E, sink_col)
        sink_cols.append(sink_col)
    first_key = jnp.where(ti > 0, 0, WINDOW)

    def scores(c, j):
        qblk = q_slabs[c][j * WINDOW:(j + 1) * WINDOW, :]
        qstack = jnp.concatenate(
            [jnp.where(lane_grp == g, qblk, 0.0) for g in range(ATTN_GROUP)],
            axis=0).astype(BF16)
        k2 = kbuf[j * WINDOW:(j + 2) * WINDOW, c * GLA_W:(c + 1) * GLA_W]
        s = _dot_nt(qstack, k2) + band_bias[...]
        if j == 0:
            s = jnp.where(kj >= first_key, s, NEG_INF)
        return s

    def attend(c, j, s):
        m = jnp.maximum(jnp.max(s, axis=-1, keepdims=True), sink_cols[c])
        e = jnp.exp2(s - m)
        den = jnp.sum(e, axis=-1, keepdims=True) + jnp.exp2(sink_cols[c] - m)
        v2 = vbuf[j * WINDOW:(j + 2) * WINDOW, c * GLA_W:(c + 1) * GLA_W]
        o = _dot(e.astype(BF16), v2) * (1.0 / den)
        out = jnp.zeros((WINDOW, GLA_W), F32)
        for g in range(ATTN_GROUP):
            out = jnp.where(lane_grp == g, o[g * WINDOW:(g + 1) * WINDOW, :], out)
        return out.astype(BF16)

    def gating_u():
        env["u"] = _gelu_tanh(env["su"])

    def gating_v():
        vn = _group_rms(_gelu_tanh(env["sv"]), svg_ref[...])
        wt = lax.broadcasted_iota(jnp.int32, (SGU_CHUNK, SGU_GROUPS * SGU_CHUNK), 0)
        ws = lax.broadcasted_iota(jnp.int32, (SGU_CHUNK, SGU_GROUPS * SGU_CHUNK), 1) % SGU_CHUNK
        w_causal = jnp.where(ws <= wt, sgu_w_ref[...], 0.0).astype(BF16)
        out_b = []
        for j in range(tt // SGU_CHUNK):
            vchunk = vn[j * SGU_CHUNK:(j + 1) * SGU_CHUNK, :]
            vblk = jnp.concatenate(
                [jnp.where(lane_grp == g, vchunk, 0.0) for g in range(SGU_GROUPS)],
                axis=0).astype(BF16)
            sg = _dot(w_causal, vblk) + sgu_b_ref[...]
            out_b.append((env["u"][j * SGU_CHUNK:(j + 1) * SGU_CHUNK, :] * sg).astype(BF16))
        env["out_b"] = jnp.concatenate(out_b, axis=0)

    def gla_q():
        qd_f = (env["cq"] * scale) * decay_q
        env["qd"] = qd_f.astype(BF16)
        env["qexp"] = [
            jnp.concatenate([jnp.where(lane_grp == hh, qd_f[r, :], 0.0) for hh in range(GLA_HEADS)],
                            axis=0).astype(BF16) for r in chunks]

    def gla_k():
        ki = (env["ck"] * decay_k_intra).astype(BF16)
        env["ks"] = (env["ck"] * decay_k_state).astype(BF16)
        env["attn"] = [_dot_nt(env["qexp"][n], ki[r, :]) for n, r in enumerate(chunks)]

    def gla_v():
        cv = env["cv"]
        env["vc"] = cv.astype(BF16)
        env["delta"] = [
            jnp.where(same256, _dot(cv[r, :].T.astype(BF16), env["ks"][r, :]), 0.0)
            for r in chunks]

    def gla_intra():
        hrows = GLA_HEADS * C
        causal = (lax.broadcasted_iota(jnp.int32, (hrows, C), 1)
                  <= lax.broadcasted_iota(jnp.int32, (hrows, C), 0) % C)
        o_intra = []
        for n, r in enumerate(chunks):
            rfull = _dot(jnp.where(causal, env["attn"][n], 0.0).astype(BF16), env["vc"][r, :])
            oi = jnp.zeros((C, GLA_W), F32)
            for hh in range(GLA_HEADS):
                oi = jnp.where(lane_grp == hh, rfull[hh * C:(hh + 1) * C, :], oi)
            o_intra.append(oi)
        env["o_intra"] = o_intra

    order = [(c, j) for c in range(ATTN_KV_HEADS) for j in range(tt // WINDOW)]
    blocks = [[] for _ in range(ATTN_KV_HEADS)]

    def out_proj(name, piece, lo):
        env[name] = _dot(piece, w_out_ref[lo:lo + MXU_WIDTH, :])

    def gla_recurrence():
        st = state[...]
        o_chunks = []
        for n, r in enumerate(chunks):
            o_chunks.append(env["o_intra"][n] + _dot_nt(env["qd"][r, :], st.astype(BF16)))
            st = st * decay_state[n] + env["delta"][n]
        state[...] = st
        env["oc"] = jnp.concatenate(o_chunks, axis=0)

    def gla_out():
        cgate = env["cg"]
        oc = _group_rms(env["oc"], og_ref[...]) * (cgate * _sigmoid(cgate))
        out_proj("y_c", oc.astype(BF16), ATTN_Q_W + SGU_W)

    between_blocks = [
        [lambda: project("cq", OFF_CQ), gating_u],
        [lambda: project("ck", OFF_CK), gating_v],
        [lambda: project("cv", OFF_CV), gla_q],
        [lambda: project("cg", OFF_CG), gla_k],
        [gla_v, gla_intra],
        [gla_recurrence, lambda: out_proj("y_b", env["out_b"], ATTN_Q_W)],
        [lambda: out_proj("y_a0", jnp.concatenate(blocks[0], axis=0), 0), gla_out],
        [],
    ]
    stride = len(order) // len(between_blocks)
    s_next = scores(*order[0])
    for i, (c, j) in enumerate(order):
        s_cur = s_next
        if i + 1 < len(order):
            s_next = scores(*order[i + 1])
        if i % stride == 0:
            for work in between_blocks[i // stride]:
                work()
        blocks[c].append(attend(c, j, s_cur))
    out_proj("y_a1", jnp.concatenate(blocks[1], axis=0), GLA_W)
    o_ref[...] = x + env["y_a0"] + env["y_b"] + env["y_a1"] + env["y_c"]


def _mixer(x, gain, w_in_t, qg, kg, sinks, svg, sgu_w, sgu_b, gup, gb, og, w_out, layer):
    B, T, _ = x.shape
    tt = MIX_TILE_T
    assert T // tt == MIX_TILES_PER_SEQ
    tile = lambda i: jnp.maximum(i - MIX_W_STEPS, 0)
    x_block = pl.BlockSpec((None, tt, D_MODEL),
                           lambda i: (tile(i) // MIX_TILES_PER_SEQ, tile(i) % MIX_TILES_PER_SEQ, 0))
    vmem = [
        _resident((1, D_MODEL)),
        pl.BlockSpec((None, MXU_WIDTH, D_MODEL),
                     lambda i: (layer, jnp.minimum(i, MIX_W_STEPS - 1), 0)),
        pl.BlockSpec((None, GLA_GATE_RANK, D_MODEL),
                     lambda i: (layer, IN_MAIN_W // GLA_GATE_RANK, 0), pipeline_mode=pl.Buffered(1)),
        _resident((1, GLA_W)),
        _resident((1, ATTN_KV_W)),
        pl.BlockSpec(memory_space=pltpu.SMEM),
        _resident((1, SGU_W)),
        _resident((SGU_CHUNK, SGU_GROUPS * SGU_CHUNK)),
        _resident((SGU_CHUNK, SGU_W)),
        _resident((GLA_GATE_RANK, GLA_W)),
        _resident((1, GLA_W)),
        _resident((1, GLA_W)),
        pl.BlockSpec((None, MXU_WIDTH, D_MODEL),
                     lambda i: (layer, jnp.minimum(i, MIX_W // MXU_WIDTH - 1), 0)),
    ]
    return pl.pallas_call(
        _mixer_kernel,
        out_shape=jax.ShapeDtypeStruct((B, T, D_MODEL), F32),
        grid=(MIX_W_STEPS + B * MIX_TILES_PER_SEQ,),
        in_specs=[x_block] + vmem,
        out_specs=x_block,
        scratch_shapes=[
            pltpu.VMEM((IN_MAIN_W, D_MODEL), BF16),
            pltpu.VMEM((GLA_GATE_RANK, D_MODEL), BF16),
            pltpu.VMEM((MIX_W, D_MODEL), BF16),
            pltpu.VMEM((WINDOW + tt, 2 * GLA_W), BF16),
            pltpu.VMEM((WINDOW + tt, 2 * GLA_W), BF16),
            pltpu.VMEM((GLA_W, GLA_W), F32),
            pltpu.VMEM((ATTN_GROUP * WINDOW, 2 * WINDOW), F32),
        ],
        compiler_params=pltpu.CompilerParams(
            dimension_semantics=("arbitrary",), vmem_limit_bytes=VMEM_LIMIT_BYTES),
        name="token_mixer",
    )(x, gain, w_in_t, w_in_t, qg, kg, sinks, svg, sgu_w, sgu_b, gup, gb, og, w_out)


def kernel(x, ffn1_norm, ffn1_w_gate, ffn1_w_up, ffn1_w_down, mix_norm, w_in, attn_q_norm,
           attn_k_norm, attn_sinks, sgu_v_norm, sgu_w, sgu_b, gla_w_gate_up, gla_b_gate,
           gla_out_norm, w_out, ffn2_norm, ffn2_w_gate, ffn2_w_up, ffn2_w_down):
    B, T, D = x.shape
    row = lambda v: v.reshape(1, -1).astype(F32)
    w_in_t = jnp.swapaxes(w_in, 1, 2)
    for l in range(DEPTH):
        x = _ffn(x.reshape(B * T, D), row(ffn1_norm[l]), ffn1_w_gate, ffn1_w_up,
                 ffn1_w_down, l).reshape(B, T, D)
        x = _mixer(
            x, row(mix_norm[l]),
            w_in_t,
            row(jnp.tile(attn_q_norm[l], ATTN_GROUP)), row(jnp.tile(attn_k_norm[l], ATTN_KV_HEADS)),
            attn_sinks[l].astype(F32), row(sgu_v_norm[l]),
            jnp.transpose(sgu_w[l], (1, 0, 2)).reshape(SGU_CHUNK, SGU_GROUPS * SGU_CHUNK).astype(F32),
            jnp.repeat(sgu_b[l].T, HEAD_DIM, axis=1).astype(F32),
            gla_w_gate_up[l].astype(BF16), row(gla_b_gate[l]),
            row(jnp.tile(gla_out_norm[l], GLA_HEADS)), w_out, l)
        x = _ffn(x.reshape(B * T, D), row(ffn2_norm[l]), ffn2_w_gate, ffn2_w_up,
                 ffn2_w_down, l).reshape(B, T, D)
    return x
```

```python
import jax
import jax.numpy as jnp
from jax import lax
from jax.experimental import pallas as pl
from jax.experimental.pallas import tpu as pltpu

F32 = jnp.float32
BF16 = jnp.bfloat16

D_MODEL = 1024
D_FF = 2816
SEQ = 4096
DEPTH = 2
HEAD_DIM = 64
ATTN_Q_HEADS = 8
ATTN_KV_HEADS = 2
ATTN_GROUP = ATTN_Q_HEADS // ATTN_KV_HEADS
WINDOW = 128
SGU_GROUPS = 4
SGU_CHUNK = 128
GLA_HEADS = 4
GLA_GATE_RANK = 16
GLA_TAU = 16.0
GLA_CHUNK = 64
RMS_EPS = 1e-6
NEG_INF = -1e30
LOG2_E = 1.4426950408889634

ATTN_Q_W = ATTN_Q_HEADS * HEAD_DIM
ATTN_KV_W = ATTN_KV_HEADS * HEAD_DIM
SGU_W = SGU_GROUPS * HEAD_DIM
GLA_W = GLA_HEADS * HEAD_DIM
OFF_AQ = 0
OFF_AK = OFF_AQ + ATTN_Q_W
OFF_AV = OFF_AK + ATTN_KV_W
OFF_SU = OFF_AV + ATTN_KV_W
OFF_SV = OFF_SU + SGU_W
OFF_CQ = OFF_SV + SGU_W
OFF_CK = OFF_CQ + GLA_W
OFF_CV = OFF_CK + GLA_W
OFF_CG = OFF_CV + GLA_W
OFF_LR = OFF_CG + GLA_W
IN_MAIN_W = OFF_LR
MIX_W = ATTN_Q_W + SGU_W + GLA_W

FFN_TILE_M = 1024
MXU_WIDTH = 256
LANES = 128
FFN_CHUNK_BOUNDS = ((0, 6 * MXU_WIDTH), (6 * MXU_WIDTH, D_FF))
MIX_TILE_T = 1024
VMEM_LIMIT_BYTES = 56 * 1024 * 1024


def _dot(a, b):
    return jnp.dot(a, b, preferred_element_type=F32)


def _dot_nt(a, b):
    return lax.dot_general(a, b, (((1,), (1,)), ((), ())), preferred_element_type=F32)


def _sigmoid(x):
    return 1.0 / (1.0 + jnp.exp(-x))


def _gelu_tanh(x):
    c = 0.7978845608028654
    return x * (0.5 * (1.0 + jnp.tanh(c * (x + 0.044715 * (x * x * x)))))


def _rms_rows(x, gain):
    ms = jnp.mean(x * x, axis=-1, keepdims=True)
    return x * lax.rsqrt(ms + RMS_EPS) * gain


def _group_ids(shape, dim, group):
    return lax.broadcasted_iota(jnp.int32, shape, dim) // group


def _group_rms(x, gain):
    low = lax.broadcasted_iota(jnp.int32, (1, LANES), 1) < HEAD_DIM
    sq = x * x
    out = []
    for s in range(0, x.shape[1], LANES):
        t = sq[:, s:s + LANES]
        ss_low = jnp.sum(jnp.where(low, t, 0.0), axis=-1, keepdims=True)
        ss_high = jnp.sum(jnp.where(low, 0.0, t), axis=-1, keepdims=True)
        ms = jnp.where(low, ss_low, ss_high) * (1.0 / HEAD_DIM)
        out.append(x[:, s:s + LANES] * lax.rsqrt(ms + RMS_EPS))
    return jnp.concatenate(out, axis=1) * gain


FFN_W_STEPS = 8
FFN_IN_SLAB = D_MODEL // FFN_W_STEPS
FFN_OUT_SLAB = D_FF // FFN_W_STEPS


def _ffn_kernel(x_ref, gain_ref, wg_ref, wu_ref, wd_ref, o_ref, wg_s, wu_s, wd_s):
    i = pl.program_id(0)

    @pl.when(i < FFN_W_STEPS)
    def _():
        for c in range(FFN_W_STEPS):
            @pl.when(i == c)
            def _(c=c):
                rows_in = slice(c * FFN_IN_SLAB, (c + 1) * FFN_IN_SLAB)
                wg_s[rows_in, :] = wg_ref[...].astype(BF16)
                wu_s[rows_in, :] = wu_ref[...].astype(BF16)
                wd_s[c * FFN_OUT_SLAB:(c + 1) * FFN_OUT_SLAB, :] = wd_ref[...].astype(BF16)

    @pl.when(i >= FFN_W_STEPS)
    def _():
        x = x_ref[...]
        h = _rms_rows(x, gain_ref[...]).astype(BF16)
        y = None
        for lo, hi in FFN_CHUNK_BOUNDS:
            gate = _dot(h, wg_s[:, lo:hi])
            up = _dot(h, wu_s[:, lo:hi])
            act = (gate * _sigmoid(gate) * up).astype(BF16)
            part = _dot(act, wd_s[lo:hi, :])
            y = part if y is None else y + part
        o_ref[...] = x + 0.5 * y


def _resident(shape):
    nd = len(shape)
    return pl.BlockSpec(shape, lambda *_: (0,) * nd, pipeline_mode=pl.Buffered(1))


def _ffn(x2d, gain, wg, wu, wd, layer):
    m = x2d.shape[0]
    tiles = m // FFN_TILE_M
    tile = lambda i: jnp.maximum(i - FFN_W_STEPS, 0)
    slab = lambda i: jnp.minimum(i, FFN_W_STEPS - 1)
    return pl.pallas_call(
        _ffn_kernel,
        out_shape=jax.ShapeDtypeStruct((m, D_MODEL), F32),
        grid=(FFN_W_STEPS + tiles,),
        in_specs=[
            pl.BlockSpec((FFN_TILE_M, D_MODEL), lambda i: (tile(i), 0)),
            _resident((1, D_MODEL)),
            pl.BlockSpec((None, FFN_IN_SLAB, D_FF), lambda i: (layer, slab(i), 0)),
            pl.BlockSpec((None, FFN_IN_SLAB, D_FF), lambda i: (layer, slab(i), 0)),
            pl.BlockSpec((None, FFN_OUT_SLAB, D_MODEL), lambda i: (layer, slab(i), 0)),
        ],
        out_specs=pl.BlockSpec((FFN_TILE_M, D_MODEL), lambda i: (tile(i), 0)),
        scratch_shapes=[
            pltpu.VMEM((D_MODEL, D_FF), BF16),
            pltpu.VMEM((D_MODEL, D_FF), BF16),
            pltpu.VMEM((D_FF, D_MODEL), BF16),
        ],
        compiler_params=pltpu.CompilerParams(
            dimension_semantics=("arbitrary",), vmem_limit_bytes=VMEM_LIMIT_BYTES),
        name="swiglu_ffn",
    )(x2d, gain, wg, wu, wd)


MIX_W_STEPS = 4
MIX_IN_SLAB = IN_MAIN_W // MIX_W_STEPS
MIX_OUT_SLAB = MIX_W // MIX_W_STEPS
MIX_TILES_PER_SEQ = SEQ // MIX_TILE_T


def _mixer_kernel(x_ref, gain_ref, w_in_t_f32_ref, w_lr_t_f32_ref, qg_ref, kg_ref, sink_ref, svg_ref,
                  sgu_w_ref, sgu_b_ref, gup_ref, gb_ref, og_ref, w_out_f32_ref,
                  o_ref, w_in_t_s, w_lr_t_s, w_out_s, *scratch):
    i = pl.program_id(0)

    @pl.when(i < MIX_W_STEPS)
    def _():
        for c in range(MIX_W_STEPS):
            @pl.when(i == c)
            def _(c=c):
                w_in_t_s[c * MIX_IN_SLAB:(c + 1) * MIX_IN_SLAB, :] = w_in_t_f32_ref[...].astype(BF16)
                w_out_s[c * MIX_OUT_SLAB:(c + 1) * MIX_OUT_SLAB, :] = w_out_f32_ref[...].astype(BF16)
                if c == 0:
                    w_lr_t_s[...] = w_lr_t_f32_ref[...].astype(BF16)

    @pl.when(i >= MIX_W_STEPS)
    def _():
        _mixer_token_step((i - MIX_W_STEPS) % MIX_TILES_PER_SEQ, x_ref, gain_ref, w_in_t_s, w_lr_t_s,
                          qg_ref, kg_ref, sink_ref, svg_ref, sgu_w_ref, sgu_b_ref, gup_ref, gb_ref,
                          og_ref, w_out_s, o_ref, *scratch)


def _mixer_token_step(ti, x_ref, gain_ref, w_in_t_ref, w_lr_t_ref, qg_ref, kg_ref, sink_ref, svg_ref,
                      sgu_w_ref, sgu_b_ref, gup_ref, gb_ref, og_ref, w_out_ref,
                      o_ref, kbuf, vbuf, state, band_bias):
    tt = MIX_TILE_T
    rows = ATTN_GROUP * WINDOW
    kj = lax.broadcasted_iota(jnp.int32, (rows, 2 * WINDOW), 1)

    @pl.when(ti > 0)
    def _():
        kbuf[0:WINDOW, :] = kbuf[tt:tt + WINDOW, :]
        vbuf[0:WINDOW, :] = vbuf[tt:tt + WINDOW, :]

    @pl.when(ti == 0)
    def _():
        kbuf[0:WINDOW, :] = jnp.zeros((WINDOW, 2 * GLA_W), BF16)
        vbuf[0:WINDOW, :] = jnp.zeros((WINDOW, 2 * GLA_W), BF16)
        state[...] = jnp.zeros_like(state)
        rel = lax.broadcasted_iota(jnp.int32, (rows, 2 * WINDOW), 0) % WINDOW + WINDOW - kj
        band_bias[...] = jnp.where((rel >= 0) & (rel < WINDOW), 0.0, NEG_INF)

    x = x_ref[...]
    h = _rms_rows(x, gain_ref[...]).astype(BF16)
    p_attn = _dot_nt(h, w_in_t_ref[OFF_AQ:OFF_SU, :])
    lr = _dot_nt(h, w_lr_t_ref[...])
    env = {}

    def project(name, lo):
        env[name] = _dot_nt(h, w_in_t_ref[lo:lo + MXU_WIDTH, :])

    same256 = (_group_ids((GLA_W, GLA_W), 0, HEAD_DIM)
               == _group_ids((GLA_W, GLA_W), 1, HEAD_DIM))
    lane_grp = _group_ids((1, GLA_W), 1, HEAD_DIM)
    scale = HEAD_DIM ** -0.5

    C = GLA_CHUNK
    chunks = [slice(n * C, (n + 1) * C) for n in range(tt // C)]
    logits = _dot(lr.astype(BF16), gup_ref[...]) + gb_ref[...]
    g_log = -(jnp.maximum(-logits, 0.0) + jnp.log(1.0 + jnp.exp(-jnp.abs(logits)))) * (1.0 / GLA_TAU)
    g_hi = g_log.astype(BF16)
    g_lo = (g_log - g_hi.astype(F32)).astype(BF16)

    q_slabs = []
    for c in range(ATTN_KV_HEADS):
        qs = p_attn[:, OFF_AQ + c * GLA_W: OFF_AQ + (c + 1) * GLA_W]
        q_slabs.append(_group_rms(qs, qg_ref[...]) * (scale * LOG2_E))
    kn = _group_rms(p_attn[:, OFF_AK:OFF_AK + ATTN_KV_W], kg_ref[...])

    project("su", OFF_SU)
    project("sv", OFF_SV)

    tri = jnp.where(lax.broadcasted_iota(jnp.int32, (C, C), 1)
                    <= lax.broadcasted_iota(jnp.int32, (C, C), 0), 1.0, 0.0).astype(BF16)
    b = jnp.concatenate([_dot(tri, g_hi[r, :]) + _dot(tri, g_lo[r, :]) for r in chunks], axis=0)
    b_last = [b[r.stop - 1:r.stop, :] for r in chunks]
    b_end = jnp.concatenate([jnp.broadcast_to(bl, (C, GLA_W)) for bl in b_last], axis=0)
    decay_q = jnp.exp(b)
    decay_k_intra = jnp.exp(-b)
    decay_k_state = jnp.exp(b_end - b)
    decay_state = [jnp.exp(bl) for bl in b_last]

    low_half = lax.broadcasted_iota(jnp.int32, (1, ATTN_KV_W), 1) < HEAD_DIM

    def replicate(a):
        swapped = pltpu.roll(a, HEAD_DIM, axis=1)
        h0 = jnp.where(low_half, a, swapped).astype(BF16)
        h1 = jnp.where(low_half, swapped, a).astype(BF16)
        return jnp.concatenate([h0, h0, h1, h1], axis=1)

    kbuf[WINDOW:WINDOW + tt, :] = replicate(kn)
    vbuf[WINDOW:WINDOW + tt, :] = replicate(p_attn[:, OFF_AV:OFF_AV + ATTN_KV_W])

    row_head = _group_ids((rows, 1), 0, WINDOW)
    sink_cols = []
    for c in range(ATTN_KV_HEADS):
        sink_col = jnp.zeros((rows, 1), F32)
        for g in range(ATTN_GROUP):
            sink_col = jnp.where(row_head == g, sink_ref[ATTN_GROUP * c + g] * LOG2_E, sink_col)
        sink_cols.append(sink_col)
    first_key = jnp.where(ti > 0, 0, WINDOW)

    def scores(c, j):
        qblk = q_slabs[c][j * WINDOW:(j + 1) * WINDOW, :]
        qstack = jnp.concatenate(
            [jnp.where(lane_grp == g, qblk, 0.0) for g in range(ATTN_GROUP)],
            axis=0).astype(BF16)
        k2 = kbuf[j * WINDOW:(j + 2) * WINDOW, c * GLA_W:(c + 1) * GLA_W]
        s = _dot_nt(qstack, k2) + band_bias[...]
        if j == 0:
            s = jnp.where(kj >= first_key, s, NEG_INF)
        return s

    def attend(c, j, s):
        m = jnp.maximum(jnp.max(s, axis=-1, keepdims=True), sink_cols[c])
        e = jnp.exp2(s - m)
        den = jnp.sum(e, axis=-1, keepdims=True) + jnp.exp2(sink_cols[c] - m)
        v2 = vbuf[j * WINDOW:(j + 2) * WINDOW, c * GLA_W:(c + 1) * GLA_W]
        o = _dot(e.astype(BF16), v2) * (1.0 / den)
        out = jnp.zeros((WINDOW, GLA_W), F32)
        for g in range(ATTN_GROUP):
            out = jnp.where(lane_grp == g, o[g * WINDOW:(g + 1) * WINDOW, :], out)
        return out.astype(BF16)

    def gating_u():
        env["u"] = _gelu_tanh(env["su"])

    def gating_v():
        vn = _group_rms(_gelu_tanh(env["sv"]), svg_ref[...])
        wt = lax.broadcasted_iota(jnp.int32, (SGU_CHUNK, SGU_GROUPS * SGU_CHUNK), 0)
        ws = lax.broadcasted_iota(jnp.int32, (SGU_CHUNK, SGU_GROUPS * SGU_CHUNK), 1) % SGU_CHUNK
        w_causal = jnp.where(ws <= wt, sgu_w_ref[...], 0.0).astype(BF16)
        out_b = []
        for j in range(tt // SGU_CHUNK):
            vchunk = vn[j * SGU_CHUNK:(j + 1) * SGU_CHUNK, :]
            vblk = jnp.concatenate(
                [jnp.where(lane_grp == g, vchunk, 0.0) for g in range(SGU_GROUPS)],
                axis=0).astype(BF16)
            sg = _dot(w_causal, vblk) + sgu_b_ref[...]
            out_b.append((env["u"][j * SGU_CHUNK:(j + 1) * SGU_CHUNK, :] * sg).astype(BF16))
        env["out_b"] = jnp.concatenate(out_b, axis=0)

    def gla_q():
        qd_f = (env["cq"] * scale) * decay_q
        env["qd"] = qd_f.astype(BF16)
        env["qexp"] = [
            jnp.concatenate([jnp.where(lane_grp == hh, qd_f[r, :], 0.0) for hh in range(GLA_HEADS)],
                            axis=0).astype(BF16) for r in chunks]

    def gla_k():
        ki = (env["ck"] * decay_k_intra).astype(BF16)
        env["ks"] = (env["ck"] * decay_k_state).astype(BF16)
        env["attn"] = [_dot_nt(env["qexp"][n], ki[r, :]) for n, r in enumerate(chunks)]

    def gla_v():
        cv = env["cv"]
        env["vc"] = cv.astype(BF16)
        env["delta"] = [
            jnp.where(same256, _dot(cv[r, :].T.astype(BF16), env["ks"][r, :]), 0.0)
            for r in chunks]

    def gla_intra():
        hrows = GLA_HEADS * C
        causal = (lax.broadcasted_iota(jnp.int32, (hrows, C), 1)
                  <= lax.broadcasted_iota(jnp.int32, (hrows, C), 0) % C)
        o_intra = []
        for n, r in enumerate(chunks):
            rfull = _dot(jnp.where(causal, env["attn"][n], 0.0).astype(BF16), env["vc"][r, :])
            oi = jnp.zeros((C, GLA_W), F32)
            for hh in range(GLA_HEADS):
                oi = jnp.where(lane_grp == hh, rfull[hh * C:(hh + 1) * C, :], oi)
            o_intra.append(oi)
        env["o_intra"] = o_intra

    order = [(c, j) for c in range(ATTN_KV_HEADS) for j in range(tt // WINDOW)]
    blocks = [[] for _ in range(ATTN_KV_HEADS)]

    def out_proj(name, piece, lo):
        env[name] = _dot(piece, w_out_ref[lo:lo + MXU_WIDTH, :])

    def gla_recurrence():
        st = state[...]
        o_chunks = []
        for n, r in enumerate(chunks):
            o_chunks.append(env["o_intra"][n] + _dot_nt(env["qd"][r, :], st.astype(BF16)))
            st = st * decay_state[n] + env["delta"][n]
        state[...] = st
        env["oc"] = jnp.concatenate(o_chunks, axis=0)

    def gla_out():
        cgate = env["cg"]
        oc = _group_rms(env["oc"], og_ref[...]) * (cgate * _sigmoid(cgate))
        out_proj("y_c", oc.astype(BF16), ATTN_Q_W + SGU_W)

    between_blocks = [
        [lambda: project("cq", OFF_CQ), gating_u],
        [lambda: project("ck", OFF_CK), gating_v],
        [lambda: project("cv", OFF_CV), gla_q],
        [lambda: project("cg", OFF_CG), gla_k],
        [gla_v, gla_intra],
        [gla_recurrence, lambda: out_proj("y_b", env["out_b"], ATTN_Q_W)],
        [lambda: out_proj("y_a0", jnp.concatenate(blocks[0], axis=0), 0), gla_out],
        [],
    ]
    stride = len(order) // len(between_blocks)
    s_next = scores(*order[0])
    for i, (c, j) in enumerate(order):
        s_cur = s_next
        if i + 1 < len(order):
            s_next = scores(*order[i + 1])
        if i % stride == 0:
            for work in between_blocks[i // stride]:
                work()
        blocks[c].append(attend(c, j, s_cur))
    out_proj("y_a1", jnp.concatenate(blocks[1], axis=0), GLA_W)
    o_ref[...] = x + env["y_a0"] + env["y_b"] + env["y_a1"] + env["y_c"]


def _mixer(x, gain, w_in_t, qg, kg, sinks, svg, sgu_w, sgu_b, gup, gb, og, w_out, layer):
    B, T, _ = x.shape
    tt = MIX_TILE_T
    assert T // tt == MIX_TILES_PER_SEQ
    tile = lambda i: jnp.maximum(i - MIX_W_STEPS, 0)
    x_block = pl.BlockSpec((None, tt, D_MODEL),
                           lambda i: (tile(i) // MIX_TILES_PER_SEQ, tile(i) % MIX_TILES_PER_SEQ, 0))
    vmem = [
        _resident((1, D_MODEL)),
        pl.BlockSpec((None, MIX_IN_SLAB, D_MODEL),
                     lambda i: (layer, jnp.minimum(i, MIX_W_STEPS - 1), 0)),
        pl.BlockSpec((None, GLA_GATE_RANK, D_MODEL),
                     lambda i: (layer, IN_MAIN_W // GLA_GATE_RANK, 0), pipeline_mode=pl.Buffered(1)),
        _resident((1, GLA_W)),
        _resident((1, ATTN_KV_W)),
        pl.BlockSpec(memory_space=pltpu.SMEM),
        _resident((1, SGU_W)),
        _resident((SGU_CHUNK, SGU_GROUPS * SGU_CHUNK)),
        _resident((SGU_CHUNK, SGU_W)),
        _resident((GLA_GATE_RANK, GLA_W)),
        _resident((1, GLA_W)),
        _resident((1, GLA_W)),
        pl.BlockSpec((None, MIX_OUT_SLAB, D_MODEL),
                     lambda i: (layer, jnp.minimum(i, MIX_W_STEPS - 1), 0)),
    ]
    return pl.pallas_call(
        _mixer_kernel,
        out_shape=jax.ShapeDtypeStruct((B, T, D_MODEL), F32),
        grid=(MIX_W_STEPS + B * MIX_TILES_PER_SEQ,),
        in_specs=[x_block] + vmem,
        out_specs=x_block,
        scratch_shapes=[
            pltpu.VMEM((IN_MAIN_W, D_MODEL), BF16),
            pltpu.VMEM((GLA_GATE_RANK, D_MODEL), BF16),
            pltpu.VMEM((MIX_W, D_MODEL), BF16),
            pltpu.VMEM((WINDOW + tt, 2 * GLA_W), BF16),
            pltpu.VMEM((WINDOW + tt, 2 * GLA_W), BF16),
            pltpu.VMEM((GLA_W, GLA_W), F32),
            pltpu.VMEM((ATTN_GROUP * WINDOW, 2 * WINDOW), F32),
        ],
        compiler_params=pltpu.CompilerParams(
            dimension_semantics=("arbitrary",), vmem_limit_bytes=VMEM_LIMIT_BYTES),
        name="token_mixer",
    )(x, gain, w_in_t, w_in_t, qg, kg, sinks, svg, sgu_w, sgu_b, gup, gb, og, w_out)


def kernel(x, ffn1_norm, ffn1_w_gate, ffn1_w_up, ffn1_w_down, mix_norm, w_in, attn_q_norm,
           attn_k_norm, attn_sinks, sgu_v_norm, sgu_w, sgu_b, gla_w_gate_up, gla_b_gate,
           gla_out_norm, w_out, ffn2_norm, ffn2_w_gate, ffn2_w_up, ffn2_w_down):
    B, T, D = x.shape
    row = lambda v: v.reshape(1, -1).astype(F32)
    w_in_t = jnp.swapaxes(w_in, 1, 2)
    for l in range(DEPTH):
        x = _ffn(x.reshape(B * T, D), row(ffn1_norm[l]), ffn1_w_gate, ffn1_w_up,
                 ffn1_w_down, l).reshape(B, T, D)
        x = _mixer(
            x, row(mix_norm[l]),
            w_in_t,
            row(jnp.tile(attn_q_norm[l], ATTN_GROUP)), row(jnp.tile(attn_k_norm[l], ATTN_KV_HEADS)),
            attn_sinks[l].astype(F32), row(sgu_v_norm[l]),
            jnp.transpose(sgu_w[l], (1, 0, 2)).reshape(SGU_CHUNK, SGU_GROUPS * SGU_CHUNK).astype(F32),
            jnp.repeat(sgu_b[l].T, HEAD_DIM, axis=1).astype(F32),
            gla_w_gate_up[l].astype(BF16), row(gla_b_gate[l]),
            row(jnp.tile(gla_out_norm[l], GLA_HEADS)), w_out, l)
        x = _ffn(x.reshape(B * T, D), row(ffn2_norm[l]), ffn2_w_gate, ffn2_w_up,
                 ffn2_w_down, l).reshape(B, T, D)
    return x
```

```python
import jax
import jax.numpy as jnp
from jax import lax
from jax.experimental import pallas as pl
from jax.experimental.pallas import tpu as pltpu

F32 = jnp.float32
BF16 = jnp.bfloat16

D_MODEL = 1024
D_FF = 2816
SEQ = 4096
DEPTH = 2
HEAD_DIM = 64
ATTN_Q_HEADS = 8
ATTN_KV_HEADS = 2
ATTN_GROUP = ATTN_Q_HEADS // ATTN_KV_HEADS
WINDOW = 128
SGU_GROUPS = 4
SGU_CHUNK = 128
GLA_HEADS = 4
GLA_GATE_RANK = 16
GLA_TAU = 16.0
GLA_CHUNK = 64
RMS_EPS = 1e-6
NEG_INF = -1e30
LOG2_E = 1.4426950408889634

ATTN_Q_W = ATTN_Q_HEADS * HEAD_DIM
ATTN_KV_W = ATTN_KV_HEADS * HEAD_DIM
SGU_W = SGU_GROUPS * HEAD_DIM
GLA_W = GLA_HEADS * HEAD_DIM
OFF_AQ = 0
OFF_AK = OFF_AQ + ATTN_Q_W
OFF_AV = OFF_AK + ATTN_KV_W
OFF_SU = OFF_AV + ATTN_KV_W
OFF_SV = OFF_SU + SGU_W
OFF_CQ = OFF_SV + SGU_W
OFF_CK = OFF_CQ + GLA_W
OFF_CV = OFF_CK + GLA_W
OFF_CG = OFF_CV + GLA_W
OFF_LR = OFF_CG + GLA_W
IN_MAIN_W = OFF_LR
MIX_W = ATTN_Q_W + SGU_W + GLA_W

FFN_TILE_M = 1024
MXU_WIDTH = 256
LANES = 128
FFN_CHUNK_BOUNDS = ((0, 6 * MXU_WIDTH), (6 * MXU_WIDTH, D_FF))
MIX_TILE_T = 1024
VMEM_LIMIT_BYTES = 56 * 1024 * 1024


def _dot(a, b):
    return jnp.dot(a, b, preferred_element_type=F32)


def _dot_nt(a, b):
    return lax.dot_general(a, b, (((1,), (1,)), ((), ())), preferred_element_type=F32)


def _sigmoid(x):
    return 1.0 / (1.0 + jnp.exp(-x))


def _gelu_tanh(x):
    c = 0.7978845608028654
    return x * (0.5 * (1.0 + jnp.tanh(c * (x + 0.044715 * (x * x * x)))))


def _rms_rows(x, gain):
    ms = jnp.mean(x * x, axis=-1, keepdims=True)
    return x * lax.rsqrt(ms + RMS_EPS) * gain


def _group_ids(shape, dim, group):
    return lax.broadcasted_iota(jnp.int32, shape, dim) // group


def _group_rms(x, gain):
    low = lax.broadcasted_iota(jnp.int32, (1, LANES), 1) < HEAD_DIM
    sq = x * x
    out = []
    for s in range(0, x.shape[1], LANES):
        t = sq[:, s:s + LANES]
        ss_low = jnp.sum(jnp.where(low, t, 0.0), axis=-1, keepdims=True)
        ss_high = jnp.sum(jnp.where(low, 0.0, t), axis=-1, keepdims=True)
        ms = jnp.where(low, ss_low, ss_high) * (1.0 / HEAD_DIM)
        out.append(x[:, s:s + LANES] * lax.rsqrt(ms + RMS_EPS))
    return jnp.concatenate(out, axis=1) * gain


FFN_W_STEPS = 8
FFN_IN_SLAB = D_MODEL // FFN_W_STEPS
FFN_OUT_SLAB = D_FF // FFN_W_STEPS


def _ffn_kernel(x_ref, gain_ref, wg_ref, wu_ref, wd_ref, o_ref, wg_s, wu_s, wd_s):
    i = pl.program_id(0)

    @pl.when(i < FFN_W_STEPS)
    def _():
        for c in range(FFN_W_STEPS):
            @pl.when(i == c)
            def _(c=c):
                rows_in = slice(c * FFN_IN_SLAB, (c + 1) * FFN_IN_SLAB)
                wg_s[rows_in, :] = wg_ref[...].astype(BF16)
                wu_s[rows_in, :] = wu_ref[...].astype(BF16)
                wd_s[c * FFN_OUT_SLAB:(c + 1) * FFN_OUT_SLAB, :] = wd_ref[...].astype(BF16)

    @pl.when(i >= FFN_W_STEPS)
    def _():
        x = x_ref[...]
        h = _rms_rows(x, gain_ref[...]).astype(BF16)
        y = None
        for lo, hi in FFN_CHUNK_BOUNDS:
            gate = _dot(h, wg_s[:, lo:hi])
            up = _dot(h, wu_s[:, lo:hi])
            act = (gate * _sigmoid(gate) * up).astype(BF16)
            part = _dot(act, wd_s[lo:hi, :])
            y = part if y is None else y + part
        o_ref[...] = x + 0.5 * y


def _resident(shape):
    nd = len(shape)
    return pl.BlockSpec(shape, lambda *_: (0,) * nd, pipeline_mode=pl.Buffered(1))


def _ffn(x2d, gain, wg, wu, wd, layer):
    m = x2d.shape[0]
    tiles = m // FFN_TILE_M
    tile = lambda i: jnp.maximum(i - FFN_W_STEPS, 0)
    slab = lambda i: jnp.minimum(i, FFN_W_STEPS - 1)
    return pl.pallas_call(
        _ffn_kernel,
        out_shape=jax.ShapeDtypeStruct((m, D_MODEL), F32),
        grid=(FFN_W_STEPS + tiles,),
        in_specs=[
            pl.BlockSpec((FFN_TILE_M, D_MODEL), lambda i: (tile(i), 0)),
            _resident((1, D_MODEL)),
            pl.BlockSpec((None, FFN_IN_SLAB, D_FF), lambda i: (layer, slab(i), 0)),
            pl.BlockSpec((None, FFN_IN_SLAB, D_FF), lambda i: (layer, slab(i), 0)),
            pl.BlockSpec((None, FFN_OUT_SLAB, D_MODEL), lambda i: (layer, slab(i), 0)),
        ],
        out_specs=pl.BlockSpec((FFN_TILE_M, D_MODEL), lambda i: (tile(i), 0)),
        scratch_shapes=[
            pltpu.VMEM((D_MODEL, D_FF), BF16),
            pltpu.VMEM((D_MODEL, D_FF), BF16),
            pltpu.VMEM((D_FF, D_MODEL), BF16),
        ],
        compiler_params=pltpu.CompilerParams(
            dimension_semantics=("arbitrary",), vmem_limit_bytes=VMEM_LIMIT_BYTES),
        name="swiglu_ffn",
    )(x2d, gain, wg, wu, wd)


MIX_W_STEPS = 4
MIX_IN_SLAB = IN_MAIN_W // MIX_W_STEPS
MIX_OUT_SLAB = MIX_W // MIX_W_STEPS
MIX_TILES_PER_SEQ = SEQ // MIX_TILE_T


def _mixer_kernel(x_ref, gain_ref, w_in_t_f32_ref, w_lr_t_f32_ref, qg_ref, kg_ref, sink_ref, svg_ref,
                  sgu_w_ref, sgu_b_ref, gup_ref, gb_ref, og_ref, w_out_f32_ref,
                  o_ref, w_in_t_s, w_lr_t_s, w_out_s, *scratch):
    i = pl.program_id(0)

    @pl.when(i < MIX_W_STEPS)
    def _():
        for c in range(MIX_W_STEPS):
            @pl.when(i == c)
            def _(c=c):
                w_in_t_s[c * MIX_IN_SLAB:(c + 1) * MIX_IN_SLAB, :] = w_in_t_f32_ref[...].astype(BF16)
                w_out_s[c * MIX_OUT_SLAB:(c + 1) * MIX_OUT_SLAB, :] = w_out_f32_ref[...].astype(BF16)
                if c == 0:
                    w_lr_t_s[...] = w_lr_t_f32_ref[...].astype(BF16)

    @pl.when(i >= MIX_W_STEPS)
    def _():
        _mixer_token_step((i - MIX_W_STEPS) % MIX_TILES_PER_SEQ, x_ref, gain_ref, w_in_t_s, w_lr_t_s,
                          qg_ref, kg_ref, sink_ref, svg_ref, sgu_w_ref, sgu_b_ref, gup_ref, gb_ref,
                          og_ref, w_out_s, o_ref, *scratch)


def _mixer_token_step(ti, x_ref, gain_ref, w_in_t_ref, w_lr_t_ref, qg_ref, kg_ref, sink_ref, svg_ref,
                      sgu_w_ref, sgu_b_ref, gup_ref, gb_ref, og_ref, w_out_ref,
                      o_ref, kbuf, vbuf, state):
    tt = MIX_TILE_T
    rows = ATTN_GROUP * WINDOW

    @pl.when(ti > 0)
    def _():
        kbuf[0:WINDOW, :] = kbuf[tt:tt + WINDOW, :]
        vbuf[0:WINDOW, :] = vbuf[tt:tt + WINDOW, :]

    @pl.when(ti == 0)
    def _():
        kbuf[0:WINDOW, :] = jnp.zeros((WINDOW, 2 * GLA_W), BF16)
        vbuf[0:WINDOW, :] = jnp.zeros((WINDOW, 2 * GLA_W), BF16)
        state[...] = jnp.zeros_like(state)

    x = x_ref[...]
    h = _rms_rows(x, gain_ref[...]).astype(BF16)
    p_attn = _dot_nt(h, w_in_t_ref[OFF_AQ:OFF_SU, :])
    lr = _dot_nt(h, w_lr_t_ref[...])
    env = {}

    def project(name, lo):
        env[name] = _dot_nt(h, w_in_t_ref[lo:lo + MXU_WIDTH, :])

    same256 = (_group_ids((GLA_W, GLA_W), 0, HEAD_DIM)
               == _group_ids((GLA_W, GLA_W), 1, HEAD_DIM))
    lane_grp = _group_ids((1, GLA_W), 1, HEAD_DIM)
    scale = HEAD_DIM ** -0.5

    C = GLA_CHUNK
    chunks = [slice(n * C, (n + 1) * C) for n in range(tt // C)]
    logits = _dot(lr.astype(BF16), gup_ref[...]) + gb_ref[...]
    g_log = -(jnp.maximum(-logits, 0.0) + jnp.log(1.0 + jnp.exp(-jnp.abs(logits)))) * (1.0 / GLA_TAU)
    g_hi = g_log.astype(BF16)
    g_lo = (g_log - g_hi.astype(F32)).astype(BF16)

    q_slabs = []
    for c in range(ATTN_KV_HEADS):
        qs = p_attn[:, OFF_AQ + c * GLA_W: OFF_AQ + (c + 1) * GLA_W]
        q_slabs.append(_group_rms(qs, qg_ref[...]) * (scale * LOG2_E))
    kn = _group_rms(p_attn[:, OFF_AK:OFF_AK + ATTN_KV_W], kg_ref[...])

    project("su", OFF_SU)
    project("sv", OFF_SV)

    tri = jnp.where(lax.broadcasted_iota(jnp.int32, (C, C), 1)
                    <= lax.broadcasted_iota(jnp.int32, (C, C), 0), 1.0, 0.0).astype(BF16)
    b = jnp.concatenate([_dot(tri, g_hi[r, :]) + _dot(tri, g_lo[r, :]) for r in chunks], axis=0)
    b_last = [b[r.stop - 1:r.stop, :] for r in chunks]
    b_end = jnp.concatenate([jnp.broadcast_to(bl, (C, GLA_W)) for bl in b_last], axis=0)
    decay_q = jnp.exp(b)
    decay_k_intra = jnp.exp(-b)
    decay_k_state = jnp.exp(b_end - b)
    decay_state = [jnp.exp(bl) for bl in b_last]

    low_half = lax.broadcasted_iota(jnp.int32, (1, ATTN_KV_W), 1) < HEAD_DIM

    def replicate(a):
        swapped = pltpu.roll(a, HEAD_DIM, axis=1)
        h0 = jnp.where(low_half, a, swapped).astype(BF16)
        h1 = jnp.where(low_half, swapped, a).astype(BF16)
        return jnp.concatenate([h0, h0, h1, h1], axis=1)

    kbuf[WINDOW:WINDOW + tt, :] = replicate(kn)
    vbuf[WINDOW:WINDOW + tt, :] = replicate(p_attn[:, OFF_AV:OFF_AV + ATTN_KV_W])

    row_head = _group_ids((rows, 1), 0, WINDOW)
    sink_cols = []
    for c in range(ATTN_KV_HEADS):
        sink_col = jnp.zeros((rows, 1), F32)
        for g in range(ATTN_GROUP):
            sink_col = jnp.where(row_head == g, sink_ref[ATTN_GROUP * c + g] * LOG2_E, sink_col)
        sink_cols.append(sink_col)
    from_prev = (lax.broadcasted_iota(jnp.int32, (rows, WINDOW), 1)
                 > lax.broadcasted_iota(jnp.int32, (rows, WINDOW), 0) % WINDOW)
    no_prev_bias = jnp.where(ti > 0, 0.0, NEG_INF)

    def scores(c, j):
        qblk = q_slabs[c][j * WINDOW:(j + 1) * WINDOW, :]
        qstack = jnp.concatenate(
            [jnp.where(lane_grp == g, qblk, 0.0) for g in range(ATTN_GROUP)],
            axis=0).astype(BF16)
        k2 = kbuf[j * WINDOW:(j + 2) * WINDOW, c * GLA_W:(c + 1) * GLA_W]
        s = _dot_nt(qstack, k2)
        s_prev = s[:, :WINDOW] + no_prev_bias if j == 0 else s[:, :WINDOW]
        return jnp.where(from_prev, s_prev, s[:, WINDOW:])

    def attend(c, j, s):
        m = jnp.maximum(jnp.max(s, axis=-1, keepdims=True), sink_cols[c])
        e = jnp.exp2(s - m)
        den = jnp.sum(e, axis=-1, keepdims=True) + jnp.exp2(sink_cols[c] - m)
        e2 = jnp.concatenate([jnp.where(from_prev, e, 0.0), jnp.where(from_prev, 0.0, e)], axis=1)
        v2 = vbuf[j * WINDOW:(j + 2) * WINDOW, c * GLA_W:(c + 1) * GLA_W]
        o = _dot(e2.astype(BF16), v2) * (1.0 / den)
        out = jnp.zeros((WINDOW, GLA_W), F32)
        for g in range(ATTN_GROUP):
            out = jnp.where(lane_grp == g, o[g * WINDOW:(g + 1) * WINDOW, :], out)
        return out.astype(BF16)

    def gating_u():
        env["u"] = _gelu_tanh(env["su"])

    def gating_v():
        vn = _group_rms(_gelu_tanh(env["sv"]), svg_ref[...])
        wt = lax.broadcasted_iota(jnp.int32, (SGU_CHUNK, SGU_GROUPS * SGU_CHUNK), 0)
        ws = lax.broadcasted_iota(jnp.int32, (SGU_CHUNK, SGU_GROUPS * SGU_CHUNK), 1) % SGU_CHUNK
        w_causal = jnp.where(ws <= wt, sgu_w_ref[...], 0.0).astype(BF16)
        out_b = []
        for j in range(tt // SGU_CHUNK):
            vchunk = vn[j * SGU_CHUNK:(j + 1) * SGU_CHUNK, :]
            vblk = jnp.concatenate(
                [jnp.where(lane_grp == g, vchunk, 0.0) for g in range(SGU_GROUPS)],
                axis=0).astype(BF16)
            sg = _dot(w_causal, vblk) + sgu_b_ref[...]
            out_b.append((env["u"][j * SGU_CHUNK:(j + 1) * SGU_CHUNK, :] * sg).astype(BF16))
        env["out_b"] = jnp.concatenate(out_b, axis=0)

    def gla_q():
        qd_f = (env["cq"] * scale) * decay_q
        env["qd"] = qd_f.astype(BF16)
        env["qexp"] = [
            jnp.concatenate([jnp.where(lane_grp == hh, qd_f[r, :], 0.0) for hh in range(GLA_HEADS)],
                            axis=0).astype(BF16) for r in chunks]

    def gla_k():
        ki = (env["ck"] * decay_k_intra).astype(BF16)
        env["ks"] = (env["ck"] * decay_k_state).astype(BF16)
        env["attn"] = [_dot_nt(env["qexp"][n], ki[r, :]) for n, r in enumerate(chunks)]

    def gla_v():
        cv = env["cv"]
        env["vc"] = cv.astype(BF16)
        env["delta"] = [
            jnp.where(same256, _dot(cv[r, :].T.astype(BF16), env["ks"][r, :]), 0.0)
            for r in chunks]

    def gla_intra():
        hrows = GLA_HEADS * C
        causal = (lax.broadcasted_iota(jnp.int32, (hrows, C), 1)
                  <= lax.broadcasted_iota(jnp.int32, (hrows, C), 0) % C)
        o_intra = []
        for n, r in enumerate(chunks):
            rfull = _dot(jnp.where(causal, env["attn"][n], 0.0).astype(BF16), env["vc"][r, :])
            oi = jnp.zeros((C, GLA_W), F32)
            for hh in range(GLA_HEADS):
                oi = jnp.where(lane_grp == hh, rfull[hh * C:(hh + 1) * C, :], oi)
            o_intra.append(oi)
        env["o_intra"] = o_intra

    order = [(c, j) for c in range(ATTN_KV_HEADS) for j in range(tt // WINDOW)]
    blocks = [[] for _ in range(ATTN_KV_HEADS)]

    def out_proj(name, piece, lo):
        env[name] = _dot(piece, w_out_ref[lo:lo + MXU_WIDTH, :])

    def gla_recurrence():
        st = state[...]
        o_chunks = []
        for n, r in enumerate(chunks):
            o_chunks.append(env["o_intra"][n] + _dot_nt(env["qd"][r, :], st.astype(BF16)))
            st = st * decay_state[n] + env["delta"][n]
        state[...] = st
        env["oc"] = jnp.concatenate(o_chunks, axis=0)

    def gla_out():
        cgate = env["cg"]
        oc = _group_rms(env["oc"], og_ref[...]) * (cgate * _sigmoid(cgate))
        out_proj("y_c", oc.astype(BF16), ATTN_Q_W + SGU_W)

    between_blocks = [
        [lambda: project("cq", OFF_CQ), gating_u],
        [lambda: project("ck", OFF_CK), gating_v],
        [lambda: project("cv", OFF_CV), gla_q],
        [lambda: project("cg", OFF_CG), gla_k],
        [gla_v, gla_intra],
        [gla_recurrence, lambda: out_proj("y_b", env["out_b"], ATTN_Q_W)],
        [lambda: out_proj("y_a0", jnp.concatenate(blocks[0], axis=0), 0), gla_out],
        [],
    ]
    stride = len(order) // len(between_blocks)
    s_next = scores(*order[0])
    for i, (c, j) in enumerate(order):
        s_cur = s_next
        if i + 1 < len(order):
            s_next = scores(*order[i + 1])
        if i % stride == 0:
            for work in between_blocks[i // stride]:
                work()
        blocks[c].append(attend(c, j, s_cur))
    out_proj("y_a1", jnp.concatenate(blocks[1], axis=0), GLA_W)
    o_ref[...] = x + env["y_a0"] + env["y_b"] + env["y_a1"] + env["y_c"]


def _mixer(x, gain, w_in_t, qg, kg, sinks, svg, sgu_w, sgu_b, gup, gb, og, w_out, layer):
    B, T, _ = x.shape
    tt = MIX_TILE_T
    assert T // tt == MIX_TILES_PER_SEQ
    tile = lambda i: jnp.maximum(i - MIX_W_STEPS, 0)
    x_block = pl.BlockSpec((None, tt, D_MODEL),
                           lambda i: (tile(i) // MIX_TILES_PER_SEQ, tile(i) % MIX_TILES_PER_SEQ, 0))
    vmem = [
        _resident((1, D_MODEL)),
        pl.BlockSpec((None, MIX_IN_SLAB, D_MODEL),
                     lambda i: (layer, jnp.minimum(i, MIX_W_STEPS - 1), 0)),
        pl.BlockSpec((None, GLA_GATE_RANK, D_MODEL),
                     lambda i: (layer, IN_MAIN_W // GLA_GATE_RANK, 0), pipeline_mode=pl.Buffered(1)),
        _resident((1, GLA_W)),
        _resident((1, ATTN_KV_W)),
        pl.BlockSpec(memory_space=pltpu.SMEM),
        _resident((1, SGU_W)),
        _resident((SGU_CHUNK, SGU_GROUPS * SGU_CHUNK)),
        _resident((SGU_CHUNK, SGU_W)),
        _resident((GLA_GATE_RANK, GLA_W)),
        _resident((1, GLA_W)),
        _resident((1, GLA_W)),
        pl.BlockSpec((None, MIX_OUT_SLAB, D_MODEL),
                     lambda i: (layer, jnp.minimum(i, MIX_W_STEPS - 1), 0)),
    ]
    return pl.pallas_call(
        _mixer_kernel,
        out_shape=jax.ShapeDtypeStruct((B, T, D_MODEL), F32),
        grid=(MIX_W_STEPS + B * MIX_TILES_PER_SEQ,),
        in_specs=[x_block] + vmem,
        out_specs=x_block,
        scratch_shapes=[
            pltpu.VMEM((IN_MAIN_W, D_MODEL), BF16),
            pltpu.VMEM((GLA_GATE_RANK, D_MODEL), BF16),
            pltpu.VMEM((MIX_W, D_MODEL), BF16),
            pltpu.VMEM((WINDOW + tt, 2 * GLA_W), BF16),
            pltpu.VMEM((WINDOW + tt, 2 * GLA_W), BF16),
            pltpu.VMEM((GLA_W, GLA_W), F32),
        ],
        compiler_params=pltpu.CompilerParams(
            dimension_semantics=("arbitrary",), vmem_limit_bytes=VMEM_LIMIT_BYTES),
        name="token_mixer",
    )(x, gain, w_in_t, w_in_t, qg, kg, sinks, svg, sgu_w, sgu_b, gup, gb, og, w_out)


def kernel(x, ffn1_norm, ffn1_w_gate, ffn1_w_up, ffn1_w_down, mix_norm, w_in, attn_q_norm,
           attn_k_norm, attn_sinks, sgu_v_norm, sgu_w, sgu_b, gla_w_gate_up, gla_b_gate,
           gla_out_norm, w_out, ffn2_norm, ffn2_w_gate, ffn2_w_up, ffn2_w_down):
    B, T, D = x.shape
    row = lambda v: v.reshape(1, -1).astype(F32)
    w_in_t = jnp.swapaxes(w_in, 1, 2)
    for l in range(DEPTH):
        x = _ffn(x.reshape(B * T, D), row(ffn1_norm[l]), ffn1_w_gate, ffn1_w_up,
                 ffn1_w_down, l).reshape(B, T, D)
        x = _mixer(
            x, row(mix_norm[l]),
            w_in_t,
            row(jnp.tile(attn_q_norm[l], ATTN_GROUP)), row(jnp.tile(attn_k_norm[l], ATTN_KV_HEADS)),
            attn_sinks[l].astype(F32), row(sgu_v_norm[l]),
            jnp.transpose(sgu_w[l], (1, 0, 2)).reshape(SGU_CHUNK, SGU_GROUPS * SGU_CHUNK).astype(F32),
            jnp.repeat(sgu_b[l].T, HEAD_DIM, axis=1).astype(F32),
            gla_w_gate_up[l].astype(BF16), row(gla_b_gate[l]),
            row(jnp.tile(gla_out_norm[l], GLA_HEADS)), w_out, l)
        x = _ffn(x.reshape(B * T, D), row(ffn2_norm[l]), ffn2_w_gate, ffn2_w_up,
                 ffn2_w_down, l).reshape(B, T, D)
    return x
```

```python
import jax
import jax.numpy as jnp
from jax import lax
from jax.experimental import pallas as pl
from jax.experimental.pallas import tpu as pltpu

F32 = jnp.float32
BF16 = jnp.bfloat16

D_MODEL = 1024
D_FF = 2816
SEQ = 4096
DEPTH = 2
HEAD_DIM = 64
ATTN_Q_HEADS = 8
ATTN_KV_HEADS = 2
ATTN_GROUP = ATTN_Q_HEADS // ATTN_KV_HEADS
WINDOW = 128
SGU_GROUPS = 4
SGU_CHUNK = 128
GLA_HEADS = 4
GLA_GATE_RANK = 16
GLA_TAU = 16.0
GLA_CHUNK = 64
RMS_EPS = 1e-6
NEG_INF = -1e30
LOG2_E = 1.4426950408889634

ATTN_Q_W = ATTN_Q_HEADS * HEAD_DIM
ATTN_KV_W = ATTN_KV_HEADS * HEAD_DIM
SGU_W = SGU_GROUPS * HEAD_DIM
GLA_W = GLA_HEADS * HEAD_DIM
OFF_AQ = 0
OFF_AK = OFF_AQ + ATTN_Q_W
OFF_AV = OFF_AK + ATTN_KV_W
OFF_SU = OFF_AV + ATTN_KV_W
OFF_SV = OFF_SU + SGU_W
OFF_CQ = OFF_SV + SGU_W
OFF_CK = OFF_CQ + GLA_W
OFF_CV = OFF_CK + GLA_W
OFF_CG = OFF_CV + GLA_W
OFF_LR = OFF_CG + GLA_W
IN_MAIN_W = OFF_LR
MIX_W = ATTN_Q_W + SGU_W + GLA_W

FFN_TILE_M = 1024
MXU_WIDTH = 256
LANES = 128
FFN_CHUNK_BOUNDS = ((0, 6 * MXU_WIDTH), (6 * MXU_WIDTH, D_FF))
MIX_TILE_T = 1024
VMEM_LIMIT_BYTES = 56 * 1024 * 1024


def _dot(a, b):
    return jnp.dot(a, b, preferred_element_type=F32)


def _dot_nt(a, b):
    return lax.dot_general(a, b, (((1,), (1,)), ((), ())), preferred_element_type=F32)


def _sigmoid(x):
    return 1.0 / (1.0 + jnp.exp(-x))


def _gelu_tanh(x):
    c = 0.7978845608028654
    return x * (0.5 * (1.0 + jnp.tanh(c * (x + 0.044715 * (x * x * x)))))


def _rms_rows(x, gain):
    ms = jnp.mean(x * x, axis=-1, keepdims=True)
    return x * lax.rsqrt(ms + RMS_EPS) * gain


def _group_ids(shape, dim, group):
    return lax.broadcasted_iota(jnp.int32, shape, dim) // group


def _group_rms(x, gain):
    low = lax.broadcasted_iota(jnp.int32, (1, LANES), 1) < HEAD_DIM
    sq = x * x
    out = []
    for s in range(0, x.shape[1], LANES):
        t = sq[:, s:s + LANES]
        ss_low = jnp.sum(jnp.where(low, t, 0.0), axis=-1, keepdims=True)
        ss_high = jnp.sum(jnp.where(low, 0.0, t), axis=-1, keepdims=True)
        ms = jnp.where(low, ss_low, ss_high) * (1.0 / HEAD_DIM)
        out.append(x[:, s:s + LANES] * lax.rsqrt(ms + RMS_EPS))
    return jnp.concatenate(out, axis=1) * gain


FFN_W_STEPS = 8
FFN_IN_SLAB = D_MODEL // FFN_W_STEPS
FFN_OUT_SLAB = D_FF // FFN_W_STEPS


def _ffn_kernel(x_ref, gain_ref, wg_ref, wu_ref, wd_ref, o_ref, wg_s, wu_s, wd_s):
    i = pl.program_id(0)

    @pl.when(i < FFN_W_STEPS)
    def _():
        for c in range(FFN_W_STEPS):
            @pl.when(i == c)
            def _(c=c):
                rows_in = slice(c * FFN_IN_SLAB, (c + 1) * FFN_IN_SLAB)
                wg_s[rows_in, :] = wg_ref[...].astype(BF16)
                wu_s[rows_in, :] = wu_ref[...].astype(BF16)
                wd_s[c * FFN_OUT_SLAB:(c + 1) * FFN_OUT_SLAB, :] = wd_ref[...].astype(BF16)

    @pl.when(i >= FFN_W_STEPS)
    def _():
        x = x_ref[...]
        h = _rms_rows(x, gain_ref[...]).astype(BF16)
        y = None
        for lo, hi in FFN_CHUNK_BOUNDS:
            gate = _dot(h, wg_s[:, lo:hi])
            up = _dot(h, wu_s[:, lo:hi])
            act = (gate * _sigmoid(gate) * up).astype(BF16)
            part = _dot(act, wd_s[lo:hi, :])
            y = part if y is None else y + part
        o_ref[...] = x + 0.5 * y


def _resident(shape, layer):
    nd = len(shape)
    return pl.BlockSpec((None,) + shape, lambda *_: (layer,) + (0,) * nd,
                        pipeline_mode=pl.Buffered(1))


def _ffn(x2d, gain, wg, wu, wd, layer):
    m = x2d.shape[0]
    tiles = m // FFN_TILE_M
    tile = lambda i: jnp.maximum(i - FFN_W_STEPS, 0)
    slab = lambda i: jnp.minimum(i, FFN_W_STEPS - 1)
    return pl.pallas_call(
        _ffn_kernel,
        out_shape=jax.ShapeDtypeStruct((m, D_MODEL), F32),
        grid=(FFN_W_STEPS + tiles,),
        in_specs=[
            pl.BlockSpec((FFN_TILE_M, D_MODEL), lambda i: (tile(i), 0)),
            _resident((1, D_MODEL), layer),
            pl.BlockSpec((None, FFN_IN_SLAB, D_FF), lambda i: (layer, slab(i), 0)),
            pl.BlockSpec((None, FFN_IN_SLAB, D_FF), lambda i: (layer, slab(i), 0)),
            pl.BlockSpec((None, FFN_OUT_SLAB, D_MODEL), lambda i: (layer, slab(i), 0)),
        ],
        out_specs=pl.BlockSpec((FFN_TILE_M, D_MODEL), lambda i: (tile(i), 0)),
        scratch_shapes=[
            pltpu.VMEM((D_MODEL, D_FF), BF16),
            pltpu.VMEM((D_MODEL, D_FF), BF16),
            pltpu.VMEM((D_FF, D_MODEL), BF16),
        ],
        compiler_params=pltpu.CompilerParams(
            dimension_semantics=("arbitrary",), vmem_limit_bytes=VMEM_LIMIT_BYTES),
        name="swiglu_ffn",
    )(x2d, gain, wg, wu, wd)


MIX_W_STEPS = 4
MIX_IN_SLAB = IN_MAIN_W // MIX_W_STEPS
MIX_OUT_SLAB = MIX_W // MIX_W_STEPS
MIX_TILES_PER_SEQ = SEQ // MIX_TILE_T


def _mixer_kernel(x_ref, gain_ref, w_in_t_f32_ref, w_lr_t_f32_ref, qg_ref, kg_ref, sink_ref, svg_ref,
                  sgu_w_ref, sgu_b_ref, gup_ref, gb_ref, og_ref, w_out_f32_ref,
                  o_ref, w_in_t_s, w_lr_t_s, w_out_s, *scratch):
    i = pl.program_id(0)

    @pl.when(i < MIX_W_STEPS)
    def _():
        for c in range(MIX_W_STEPS):
            @pl.when(i == c)
            def _(c=c):
                w_in_t_s[c * MIX_IN_SLAB:(c + 1) * MIX_IN_SLAB, :] = w_in_t_f32_ref[...].astype(BF16)
                w_out_s[c * MIX_OUT_SLAB:(c + 1) * MIX_OUT_SLAB, :] = w_out_f32_ref[...].astype(BF16)
                if c == 0:
                    w_lr_t_s[...] = w_lr_t_f32_ref[...].astype(BF16)

    @pl.when(i >= MIX_W_STEPS)
    def _():
        _mixer_token_step((i - MIX_W_STEPS) % MIX_TILES_PER_SEQ, x_ref, gain_ref, w_in_t_s, w_lr_t_s,
                          qg_ref, kg_ref, sink_ref, svg_ref, sgu_w_ref, sgu_b_ref, gup_ref, gb_ref,
                          og_ref, w_out_s, o_ref, *scratch)


def _mixer_token_step(ti, x_ref, gain_ref, w_in_t_ref, w_lr_t_ref, qg_ref, kg_ref, sink_ref, svg_ref,
                      sgu_w_ref, sgu_b_ref, gup_ref, gb_ref, og_ref, w_out_ref,
                      o_ref, kbuf, vbuf, state):
    tt = MIX_TILE_T
    rows = ATTN_GROUP * WINDOW

    @pl.when(ti > 0)
    def _():
        kbuf[0:WINDOW, :] = kbuf[tt:tt + WINDOW, :]
        vbuf[0:WINDOW, :] = vbuf[tt:tt + WINDOW, :]

    @pl.when(ti == 0)
    def _():
        kbuf[0:WINDOW, :] = jnp.zeros((WINDOW, 2 * GLA_W), BF16)
        vbuf[0:WINDOW, :] = jnp.zeros((WINDOW, 2 * GLA_W), BF16)
        state[...] = jnp.zeros_like(state)

    x = x_ref[...]
    h = _rms_rows(x, gain_ref[...]).astype(BF16)
    p_attn = _dot_nt(h, w_in_t_ref[OFF_AQ:OFF_SU, :])
    lr = _dot_nt(h, w_lr_t_ref[...])
    env = {}

    def project(name, lo):
        env[name] = _dot_nt(h, w_in_t_ref[lo:lo + MXU_WIDTH, :])

    same256 = (_group_ids((GLA_W, GLA_W), 0, HEAD_DIM)
               == _group_ids((GLA_W, GLA_W), 1, HEAD_DIM))
    lane_grp = _group_ids((1, GLA_W), 1, HEAD_DIM)
    scale = HEAD_DIM ** -0.5

    C = GLA_CHUNK
    chunks = [slice(n * C, (n + 1) * C) for n in range(tt // C)]
    logits = _dot(lr.astype(BF16), gup_ref[...]) + gb_ref[...]
    g_log = -(jnp.maximum(-logits, 0.0) + jnp.log(1.0 + jnp.exp(-jnp.abs(logits)))) * (1.0 / GLA_TAU)
    g_hi = g_log.astype(BF16)
    g_lo = (g_log - g_hi.astype(F32)).astype(BF16)

    q_slabs = []
    for c in range(ATTN_KV_HEADS):
        qs = p_attn[:, OFF_AQ + c * GLA_W: OFF_AQ + (c + 1) * GLA_W]
        q_slabs.append(_group_rms(qs, qg_ref[...]) * (scale * LOG2_E))
    kn = _group_rms(p_attn[:, OFF_AK:OFF_AK + ATTN_KV_W], kg_ref[...])

    project("su", OFF_SU)
    project("sv", OFF_SV)

    tri = jnp.where(lax.broadcasted_iota(jnp.int32, (C, C), 1)
                    <= lax.broadcasted_iota(jnp.int32, (C, C), 0), 1.0, 0.0).astype(BF16)
    b = jnp.concatenate([_dot(tri, g_hi[r, :]) + _dot(tri, g_lo[r, :]) for r in chunks], axis=0)
    b_last = [b[r.stop - 1:r.stop, :] for r in chunks]
    b_end = jnp.concatenate([jnp.broadcast_to(bl, (C, GLA_W)) for bl in b_last], axis=0)
    decay_q = jnp.exp(b)
    decay_k_intra = jnp.exp(-b)
    decay_k_state = jnp.exp(b_end - b)
    decay_state = [jnp.exp(bl) for bl in b_last]

    low_half = lax.broadcasted_iota(jnp.int32, (1, ATTN_KV_W), 1) < HEAD_DIM

    def replicate(a):
        swapped = pltpu.roll(a, HEAD_DIM, axis=1)
        h0 = jnp.where(low_half, a, swapped).astype(BF16)
        h1 = jnp.where(low_half, swapped, a).astype(BF16)
        return jnp.concatenate([h0, h0, h1, h1], axis=1)

    kbuf[WINDOW:WINDOW + tt, :] = replicate(kn)
    vbuf[WINDOW:WINDOW + tt, :] = replicate(p_attn[:, OFF_AV:OFF_AV + ATTN_KV_W])

    row_head = _group_ids((rows, 1), 0, WINDOW)
    sink_cols = []
    for c in range(ATTN_KV_HEADS):
        sink_col = jnp.zeros((rows, 1), F32)
        for g in range(ATTN_GROUP):
            sink_col = jnp.where(row_head == g, sink_ref[ATTN_GROUP * c + g] * LOG2_E, sink_col)
        sink_cols.append(sink_col)
    from_prev = (lax.broadcasted_iota(jnp.int32, (rows, WINDOW), 1)
                 > lax.broadcasted_iota(jnp.int32, (rows, WINDOW), 0) % WINDOW)
    no_prev_bias = jnp.where(ti > 0, 0.0, NEG_INF)

    def scores(c, j):
        qblk = q_slabs[c][j * WINDOW:(j + 1) * WINDOW, :]
        qstack = jnp.concatenate(
            [jnp.where(lane_grp == g, qblk, 0.0) for g in range(ATTN_GROUP)],
            axis=0).astype(BF16)
        k2 = kbuf[j * WINDOW:(j + 2) * WINDOW, c * GLA_W:(c + 1) * GLA_W]
        s = _dot_nt(qstack, k2)
        s_prev = s[:, :WINDOW] + no_prev_bias if j == 0 else s[:, :WINDOW]
        return jnp.where(from_prev, s_prev, s[:, WINDOW:])

    def attend(c, j, s):
        m = jnp.maximum(jnp.max(s, axis=-1, keepdims=True), sink_cols[c])
        e = jnp.exp2(s - m)
        den = jnp.sum(e, axis=-1, keepdims=True) + jnp.exp2(sink_cols[c] - m)
        e2 = jnp.concatenate([jnp.where(from_prev, e, 0.0), jnp.where(from_prev, 0.0, e)], axis=1)
        v2 = vbuf[j * WINDOW:(j + 2) * WINDOW, c * GLA_W:(c + 1) * GLA_W]
        o = _dot(e2.astype(BF16), v2) * (1.0 / den)
        out = jnp.zeros((WINDOW, GLA_W), F32)
        for g in range(ATTN_GROUP):
            out = jnp.where(lane_grp == g, o[g * WINDOW:(g + 1) * WINDOW, :], out)
        return out.astype(BF16)

    def gating_u():
        env["u"] = _gelu_tanh(env["su"])

    def gating_v():
        vn = _group_rms(_gelu_tanh(env["sv"]), svg_ref[...])
        wt = lax.broadcasted_iota(jnp.int32, (SGU_CHUNK, SGU_GROUPS * SGU_CHUNK), 0)
        ws = lax.broadcasted_iota(jnp.int32, (SGU_CHUNK, SGU_GROUPS * SGU_CHUNK), 1) % SGU_CHUNK
        w_causal = jnp.where(ws <= wt, sgu_w_ref[...], 0.0).astype(BF16)
        out_b = []
        for j in range(tt // SGU_CHUNK):
            vchunk = vn[j * SGU_CHUNK:(j + 1) * SGU_CHUNK, :]
            vblk = jnp.concatenate(
                [jnp.where(lane_grp == g, vchunk, 0.0) for g in range(SGU_GROUPS)],
                axis=0).astype(BF16)
            sg = _dot(w_causal, vblk) + sgu_b_ref[...]
            out_b.append((env["u"][j * SGU_CHUNK:(j + 1) * SGU_CHUNK, :] * sg).astype(BF16))
        env["out_b"] = jnp.concatenate(out_b, axis=0)

    def gla_q():
        qd_f = (env["cq"] * scale) * decay_q
        env["qd"] = qd_f.astype(BF16)
        env["qexp"] = [
            jnp.concatenate([jnp.where(lane_grp == hh, qd_f[r, :], 0.0) for hh in range(GLA_HEADS)],
                            axis=0).astype(BF16) for r in chunks]

    def gla_k():
        ki = (env["ck"] * decay_k_intra).astype(BF16)
        env["ks"] = (env["ck"] * decay_k_state).astype(BF16)
        env["attn"] = [_dot_nt(env["qexp"][n], ki[r, :]) for n, r in enumerate(chunks)]

    def gla_v():
        cv = env["cv"]
        env["vc"] = cv.astype(BF16)
        env["delta"] = [
            jnp.where(same256, _dot(cv[r, :].T.astype(BF16), env["ks"][r, :]), 0.0)
            for r in chunks]

    def gla_intra():
        hrows = GLA_HEADS * C
        causal = (lax.broadcasted_iota(jnp.int32, (hrows, C), 1)
                  <= lax.broadcasted_iota(jnp.int32, (hrows, C), 0) % C)
        o_intra = []
        for n, r in enumerate(chunks):
            rfull = _dot(jnp.where(causal, env["attn"][n], 0.0).astype(BF16), env["vc"][r, :])
            oi = jnp.zeros((C, GLA_W), F32)
            for hh in range(GLA_HEADS):
                oi = jnp.where(lane_grp == hh, rfull[hh * C:(hh + 1) * C, :], oi)
            o_intra.append(oi)
        env["o_intra"] = o_intra

    order = [(c, j) for c in range(ATTN_KV_HEADS) for j in range(tt // WINDOW)]
    blocks = [[] for _ in range(ATTN_KV_HEADS)]

    def out_proj(name, piece, lo):
        env[name] = _dot(piece, w_out_ref[lo:lo + MXU_WIDTH, :])

    def gla_recurrence():
        st = state[...]
        o_chunks = []
        for n, r in enumerate(chunks):
            o_chunks.append(env["o_intra"][n] + _dot_nt(env["qd"][r, :], st.astype(BF16)))
            st = st * decay_state[n] + env["delta"][n]
        state[...] = st
        env["oc"] = jnp.concatenate(o_chunks, axis=0)

    def gla_out():
        cgate = env["cg"]
        oc = _group_rms(env["oc"], og_ref[...]) * (cgate * _sigmoid(cgate))
        out_proj("y_c", oc.astype(BF16), ATTN_Q_W + SGU_W)

    between_blocks = [
        [lambda: project("cq", OFF_CQ), gating_u],
        [lambda: project("ck", OFF_CK), gating_v],
        [lambda: project("cv", OFF_CV), gla_q],
        [lambda: project("cg", OFF_CG), gla_k],
        [gla_v, gla_intra],
        [gla_recurrence, lambda: out_proj("y_b", env["out_b"], ATTN_Q_W)],
        [lambda: out_proj("y_a0", jnp.concatenate(blocks[0], axis=0), 0), gla_out],
        [],
    ]
    stride = len(order) // len(between_blocks)
    s_next = scores(*order[0])
    for i, (c, j) in enumerate(order):
        s_cur = s_next
        if i + 1 < len(order):
            s_next = scores(*order[i + 1])
        if i % stride == 0:
            for work in between_blocks[i // stride]:
                work()
        blocks[c].append(attend(c, j, s_cur))
    out_proj("y_a1", jnp.concatenate(blocks[1], axis=0), GLA_W)
    o_ref[...] = x + env["y_a0"] + env["y_b"] + env["y_a1"] + env["y_c"]


def _mixer(x, gain, w_in_t, qg, kg, sinks, svg, sgu_w, sgu_b, gup, gb, og, w_out, layer):
    B, T, _ = x.shape
    tt = MIX_TILE_T
    assert T // tt == MIX_TILES_PER_SEQ
    tile = lambda i: jnp.maximum(i - MIX_W_STEPS, 0)
    x_block = pl.BlockSpec((None, tt, D_MODEL),
                           lambda i: (tile(i) // MIX_TILES_PER_SEQ, tile(i) % MIX_TILES_PER_SEQ, 0))
    vmem = [
        _resident((1, D_MODEL), layer),
        pl.BlockSpec((None, MIX_IN_SLAB, D_MODEL),
                     lambda i: (layer, jnp.minimum(i, MIX_W_STEPS - 1), 0)),
        pl.BlockSpec((None, GLA_GATE_RANK, D_MODEL),
                     lambda i: (layer, IN_MAIN_W // GLA_GATE_RANK, 0), pipeline_mode=pl.Buffered(1)),
        _resident((1, GLA_W), layer),
        _resident((1, ATTN_KV_W), layer),
        pl.BlockSpec(memory_space=pltpu.SMEM),
        _resident((1, SGU_W), layer),
        _resident((SGU_CHUNK, SGU_GROUPS * SGU_CHUNK), layer),
        _resident((SGU_CHUNK, SGU_W), layer),
        _resident((GLA_GATE_RANK, GLA_W), layer),
        _resident((1, GLA_W), layer),
        _resident((1, GLA_W), layer),
        pl.BlockSpec((None, MIX_OUT_SLAB, D_MODEL),
                     lambda i: (layer, jnp.minimum(i, MIX_W_STEPS - 1), 0)),
    ]
    return pl.pallas_call(
        _mixer_kernel,
        out_shape=jax.ShapeDtypeStruct((B, T, D_MODEL), F32),
        grid=(MIX_W_STEPS + B * MIX_TILES_PER_SEQ,),
        in_specs=[x_block] + vmem,
        out_specs=x_block,
        scratch_shapes=[
            pltpu.VMEM((IN_MAIN_W, D_MODEL), BF16),
            pltpu.VMEM((GLA_GATE_RANK, D_MODEL), BF16),
            pltpu.VMEM((MIX_W, D_MODEL), BF16),
            pltpu.VMEM((WINDOW + tt, 2 * GLA_W), BF16),
            pltpu.VMEM((WINDOW + tt, 2 * GLA_W), BF16),
            pltpu.VMEM((GLA_W, GLA_W), F32),
        ],
        compiler_params=pltpu.CompilerParams(
            dimension_semantics=("arbitrary",), vmem_limit_bytes=VMEM_LIMIT_BYTES),
        name="token_mixer",
    )(x, gain, w_in_t, w_in_t, qg, kg, sinks[layer], svg, sgu_w, sgu_b, gup, gb, og, w_out)


def kernel(x, ffn1_norm, ffn1_w_gate, ffn1_w_up, ffn1_w_down, mix_norm, w_in, attn_q_norm,
           attn_k_norm, attn_sinks, sgu_v_norm, sgu_w, sgu_b, gla_w_gate_up, gla_b_gate,
           gla_out_norm, w_out, ffn2_norm, ffn2_w_gate, ffn2_w_up, ffn2_w_down):
    B, T, D = x.shape
    rows = lambda v, reps=1: jnp.tile(v.astype(F32), (1, reps)).reshape(DEPTH, 1, -1)
    w_in_t = jnp.swapaxes(w_in, 1, 2)
    mixer_params = (
        rows(mix_norm), w_in_t, rows(attn_q_norm, ATTN_GROUP), rows(attn_k_norm, ATTN_KV_HEADS),
        attn_sinks.astype(F32), rows(sgu_v_norm),
        jnp.transpose(sgu_w, (0, 2, 1, 3)).reshape(DEPTH, SGU_CHUNK, SGU_GROUPS * SGU_CHUNK).astype(F32),
        jnp.repeat(jnp.swapaxes(sgu_b, 1, 2), HEAD_DIM, axis=2).astype(F32),
        gla_w_gate_up.astype(BF16), rows(gla_b_gate), rows(gla_out_norm, GLA_HEADS), w_out)
    ffn1 = (rows(ffn1_norm), ffn1_w_gate, ffn1_w_up, ffn1_w_down)
    ffn2 = (rows(ffn2_norm), ffn2_w_gate, ffn2_w_up, ffn2_w_down)
    for l in range(DEPTH):
        x = _ffn(x.reshape(B * T, D), *ffn1, l).reshape(B, T, D)
        x = _mixer(x, *mixer_params, l)
        x = _ffn(x.reshape(B * T, D), *ffn2, l).reshape(B, T, D)
    return x
```
